```python
import math
import jax, jax.numpy as jnp
from jax import lax
import numpy as np


D_MODEL = 1024
BATCH = 2
SEQ = 16384
DEPTH = 4
DEC_BATCH = 1
DEC_SEQ = 16384
PAST_LEN = 128

CHUNK = 128
Q_BLOCK = 128
EPS = 1e-6
A_GROUPS = 4
A_GROUP_DIM = 64
A_WIDTH = A_GROUPS * A_GROUP_DIM
B_HEADS = 6
B_Q_LORA = 384
B_KV_LORA = 256
B_NOPE = 64
B_ROPE = 32
B_V = 64
B_WIDTH = B_HEADS * B_V
ROPE_BASE = 10000.0
C_HEADS = 6
C_QK = 32
C_V = 2 * C_QK
C_WIDTH = C_HEADS * C_V
REL_BUCKETS = 32
REL_MAX_DIST = 128
MIX_WIDTH = A_WIDTH + B_WIDTH + C_WIDTH
IN_SPLITS = (A_WIDTH, A_WIDTH, B_Q_LORA, B_KV_LORA, B_ROPE,
             C_HEADS * 2 * C_QK, C_HEADS * 2 * C_QK, C_HEADS * C_V)
IN_WIDTH = sum(IN_SPLITS)
N_EXPERTS = 32
TOP_K = 4
D_FF = 1024
SWIGLU_LIMIT = 7.0
SWIGLU_ALPHA = 1.702
MOE_BLOCK = 128

kernel_name = 'hybrid_gmlp_mla_diffattn_moe_encoder'


def rms_norm(x, g):
    xf = x.astype(jnp.float32)
    y = xf * lax.rsqrt(jnp.mean(xf * xf, -1, keepdims=True) + EPS)
    return (y * g.astype(jnp.float32)).astype(x.dtype)


def layer_norm(x, g, b):
    xf = x.astype(jnp.float32)
    mu = jnp.mean(xf, -1, keepdims=True)
    xc = xf - mu
    y = xc * lax.rsqrt(jnp.mean(xc * xc, -1, keepdims=True) + EPS)
    return (y * g.astype(jnp.float32) + b.astype(jnp.float32)).astype(x.dtype)


def rope(x, cos, sin):
    half = x.shape[-1] // 2
    shape = (1, cos.shape[0]) + (1,) * (x.ndim - 3) + (half,)
    c = cos.reshape(shape).astype(x.dtype)
    s = sin.reshape(shape).astype(x.dtype)
    x1, x2 = x[..., :half], x[..., half:]
    return jnp.concatenate([x1 * c - x2 * s, x1 * s + x2 * c], -1)


def t5_bucket(rel):
    half = REL_BUCKETS // 2
    max_exact = half // 2
    base = jnp.where(rel > 0, half, 0)
    n = jnp.abs(rel)
    nf = jnp.maximum(n, 1).astype(jnp.float32)
    large = max_exact + (jnp.log(nf / max_exact) / math.log(REL_MAX_DIST / max_exact)
                         * (half - max_exact)).astype(jnp.int32)
    large = jnp.minimum(large, half - 1)
    return base + jnp.where(n < max_exact, n, large)


def query_blocks(q):
    b, s = q.shape[:2]
    return jnp.moveaxis(q.reshape((b, s // Q_BLOCK, Q_BLOCK) + q.shape[2:]), 1, 0)


def unblock(o):
    nb, b, qb = o.shape[:3]
    return jnp.moveaxis(o, 0, 1).reshape((b, nb * qb) + o.shape[3:])


def gmlp_mixer(u_pre, v_pre, ln_g, ln_b, ws, bs):
    u = jax.nn.gelu(u_pre)
    v = layer_norm(jax.nn.gelu(v_pre), ln_g, ln_b)
    b, s, _ = u.shape
    v = v.reshape(b, s // CHUNK, CHUNK, A_GROUPS, A_GROUP_DIM)
    sv = jnp.einsum('gpq,bnqgc->bnpgc', ws, v) + bs.T[None, None, :, :, None]
    return u * sv.reshape(b, s, A_WIDTH)


def mla_mixer(c_q, c_kv, k_r, q_norm_g, w_uq, kv_norm_g, w_ukv):
    b, s, _ = c_q.shape
    pos = jnp.arange(s, dtype=jnp.float32)
    inv = 1.0 / (ROPE_BASE ** (jnp.arange(B_ROPE // 2, dtype=jnp.float32) / (B_ROPE // 2)))
    ang = pos[:, None] * inv[None, :]
    cos, sin = jnp.cos(ang), jnp.sin(ang)
    q = (rms_norm(c_q, q_norm_g) @ w_uq).reshape(b, s, B_HEADS, B_NOPE + B_ROPE)
    q = jnp.concatenate([q[..., :B_NOPE], rope(q[..., B_NOPE:], cos, sin)], -1)
    kv = (rms_norm(c_kv, kv_norm_g) @ w_ukv).reshape(b, s, B_HEADS, B_NOPE + B_V)
    k_rope = rope(k_r, cos, sin)
    k = jnp.concatenate([kv[..., :B_NOPE],
                         jnp.broadcast_to(k_rope[:, :, None, :], (b, s, B_HEADS, B_ROPE))], -1)
    v = kv[..., B_NOPE:]
    scale = (B_NOPE + B_ROPE) ** -0.5

    def block(qb):
        sc = jnp.einsum('bqhd,bkhd->bhqk', qb, k).astype(jnp.float32) * scale
        p = jax.nn.softmax(sc, -1).astype(v.dtype)
        return jnp.einsum('bhqk,bkhd->bqhd', p, v)

    o = unblock(lax.map(block, query_blocks(q)))
    return o.reshape(b, s, B_WIDTH)


def diff_mixer(q, k, v, lq1, lk1, lq2, lk2, subln_g, rel_bias, lam_init):
    b, s, _ = q.shape
    q = q.reshape(b, s, C_HEADS, 2, C_QK)
    k = k.reshape(b, s, C_HEADS, 2, C_QK)
    v = v.reshape(b, s, C_HEADS, C_V)
    f32 = jnp.float32
    lam = (jnp.exp(jnp.sum(lq1.astype(f32) * lk1.astype(f32)))
           - jnp.exp(jnp.sum(lq2.astype(f32) * lk2.astype(f32))) + lam_init)
    scale = C_QK ** -0.5
    k_pos = jnp.arange(s)
    nb = s // Q_BLOCK

    def block(args):
        qb, bi = args
        q_pos = bi * Q_BLOCK + jnp.arange(Q_BLOCK)
        bias = rel_bias[t5_bucket(k_pos[None, :] - q_pos[:, None])]
        bias = jnp.transpose(bias, (2, 0, 1)).astype(f32)
        sc = jnp.einsum('bqhjd,bkhjd->bhjqk', qb, k).astype(f32) * scale + bias[None, :, None]
        p = jax.nn.softmax(sc, -1)
        w = (p[:, :, 0] - lam * p[:, :, 1]).astype(v.dtype)
        return jnp.einsum('bhqk,bkhe->bqhe', w, v)

    o = unblock(lax.map(block, (query_blocks(q), jnp.arange(nb))))
    o = rms_norm(o, subln_g) * (1.0 - lam_init)
    return o.reshape(b, s, C_WIDTH)


def clamped_swiglu(h):
    gate, up = h[..., :D_FF], h[..., D_FF:]
    gate = jnp.minimum(gate, SWIGLU_LIMIT)
    up = jnp.clip(up, -SWIGLU_LIMIT, SWIGLU_LIMIT)
    return (up + 1.0) * (gate * jax.nn.sigmoid(SWIGLU_ALPHA * gate))


def moe(h, router_w, router_b, w1, b1, w2, b2):
    b, s, d = h.shape
    n = b * s
    t = h.reshape(n, d)
    logits = (t @ router_w).astype(jnp.float32) + router_b.astype(jnp.float32)
    top_v, top_i = lax.top_k(logits, TOP_K)
    gates = jax.nn.softmax(top_v, -1)
    flat_e = top_i.reshape(-1)
    flat_g = gates.reshape(-1)
    flat_t = jnp.arange(n * TOP_K, dtype=jnp.int32) // TOP_K
    order = jnp.argsort(flat_e)
    se = flat_e[order]
    counts = jnp.bincount(flat_e, length=N_EXPERTS)
    padded = ((counts + MOE_BLOCK - 1) // MOE_BLOCK) * MOE_BLOCK
    start = jnp.cumsum(counts) - counts
    pend = jnp.cumsum(padded)
    pstart = pend - padded
    dest = pstart[se] + (jnp.arange(n * TOP_K) - start[se])
    n_blocks = (n * TOP_K + MOE_BLOCK - 1) // MOE_BLOCK + N_EXPERTS
    p_len = n_blocks * MOE_BLOCK
    tok = jnp.full((p_len,), n, jnp.int32).at[dest].set(flat_t[order])
    gate = jnp.zeros((p_len,), jnp.float32).at[dest].set(flat_g[order])
    blk_e = jnp.minimum(jnp.searchsorted(pend, jnp.arange(n_blocks) * MOE_BLOCK, side='right'),
                        N_EXPERTS - 1)
    t_pad = jnp.concatenate([t, jnp.zeros((1, d), t.dtype)], 0)
    xb = t_pad[tok].reshape(n_blocks, MOE_BLOCK, d)

    def expert_block(args):
        xblk, e = args
        return clamped_swiglu(xblk @ w1[e] + b1[e]) @ w2[e] + b2[e]

    yb = lax.map(expert_block, (xb, blk_e)).reshape(p_len, d)
    out = jnp.zeros((n + 1, d), jnp.float32).at[tok].add(yb.astype(jnp.float32) * gate[:, None])[:n]
    return out.astype(h.dtype).reshape(b, s, d)


def trunk(x, c, ada_w, ada_b, norm1_g, w_in, gmlp_ln_g, gmlp_ln_b, gmlp_ws, gmlp_bs,
          mla_q_norm_g, mla_w_uq, mla_kv_norm_g, mla_w_ukv,
          diff_lq1, diff_lk1, diff_lq2, diff_lk2, diff_subln_g,
          w_out, norm2_g, router_w, router_b, moe_w1, moe_b1, moe_w2, moe_b2,
          rel_bias, final_g):
    cs = jax.nn.silu(c)
    offs = [int(o) for o in np.cumsum(IN_SPLITS)[:-1]]
    for l in range(DEPTH):
        mod = cs @ ada_w[l] + ada_b[l]
        sh1, sc1, g1, sh2, sc2, g2 = [m[:, None, :] for m in jnp.split(mod, 6, -1)]
        h = rms_norm(x, norm1_g[l]) * (1.0 + sc1) + sh1
        z = h @ w_in[l]
        a_u, a_v, b_cq, b_ckv, b_kr, c_q, c_k, c_v = jnp.split(z, offs, -1)
        lam_init = 0.8 - 0.6 * math.exp(-0.3 * l)
        mix = jnp.concatenate([
            gmlp_mixer(a_u, a_v, gmlp_ln_g[l], gmlp_ln_b[l], gmlp_ws[l], gmlp_bs[l]),
            mla_mixer(b_cq, b_ckv, b_kr, mla_q_norm_g[l], mla_w_uq[l], mla_kv_norm_g[l], mla_w_ukv[l]),
            diff_mixer(c_q, c_k, c_v, diff_lq1[l], diff_lk1[l], diff_lq2[l], diff_lk2[l],
                       diff_subln_g[l], rel_bias, lam_init),
        ], -1)
        x = x + g1 * (mix @ w_out[l])
        h2 = rms_norm(x, norm2_g[l]) * (1.0 + sc2) + sh2
        x = x + g2 * moe(h2, router_w[l], router_b[l], moe_w1[l], moe_b1[l], moe_w2[l], moe_b2[l])
    return rms_norm(x, final_g)


def setup_inputs(seed: int = 0) -> dict:
    key = jax.random.key(seed)
    ks = jax.random.split(key, 40)
    D = D_MODEL

    def nrm(k, shape, scale):
        return jax.random.normal(k, shape, jnp.float32) * scale

    return {
        'x_prompt': nrm(ks[0], (BATCH, SEQ, D), 1.0),
        'x_sample': nrm(ks[1], (DEC_BATCH, DEC_SEQ, D), 1.0),
        'c_prompt': nrm(ks[2], (BATCH, D), 1.0),
        'c_sample': nrm(ks[3], (DEC_BATCH, D), 1.0),
        'ada_w': nrm(ks[4], (DEPTH, D, 6 * D), 0.5 * D ** -0.5),
        'ada_b': nrm(ks[5], (DEPTH, 6 * D), 0.02),
        'norm1_g': 1.0 + nrm(ks[6], (DEPTH, D), 0.02),
        'w_in': nrm(ks[7], (DEPTH, D, IN_WIDTH), D ** -0.5),
        'gmlp_ln_g': 1.0 + nrm(ks[8], (DEPTH, A_WIDTH), 0.02),
        'gmlp_ln_b': nrm(ks[9], (DEPTH, A_WIDTH), 0.02),
        'gmlp_ws': nrm(ks[10], (DEPTH, A_GROUPS, CHUNK, CHUNK), CHUNK ** -0.5),
        'gmlp_bs': 1.0 + nrm(ks[11], (DEPTH, A_GROUPS, CHUNK), 0.02),
        'mla_q_norm_g': 1.0 + nrm(ks[12], (DEPTH, B_Q_LORA), 0.02),
        'mla_w_uq': nrm(ks[13], (DEPTH, B_Q_LORA, B_HEADS * (B_NOPE + B_ROPE)), B_Q_LORA ** -0.5),
        'mla_kv_norm_g': 1.0 + nrm(ks[14], (DEPTH, B_KV_LORA), 0.02),
        'mla_w_ukv': nrm(ks[15], (DEPTH, B_KV_LORA, B_HEADS * (B_NOPE + B_V)), B_KV_LORA ** -0.5),
        'diff_lq1': nrm(ks[16], (DEPTH, C_QK), 0.1),
        'diff_lk1': nrm(ks[17], (DEPTH, C_QK), 0.1),
        'diff_lq2': nrm(ks[18], (DEPTH, C_QK), 0.1),
        'diff_lk2': nrm(ks[19], (DEPTH, C_QK), 0.1),
        'diff_subln_g': 1.0 + nrm(ks[20], (DEPTH, C_V), 0.02),
        'w_out': nrm(ks[21], (DEPTH, MIX_WIDTH, D), MIX_WIDTH ** -0.5),
        'norm2_g': 1.0 + nrm(ks[22], (DEPTH, D), 0.02),
        'router_w': nrm(ks[23], (DEPTH, D, N_EXPERTS), D ** -0.5),
        'router_b': nrm(ks[24], (DEPTH, N_EXPERTS), 0.01),
        'moe_w1': nrm(ks[25], (DEPTH, N_EXPERTS, D, 2 * D_FF), D ** -0.5),
        'moe_b1': nrm(ks[26], (DEPTH, N_EXPERTS, 2 * D_FF), 0.02),
        'moe_w2': nrm(ks[27], (DEPTH, N_EXPERTS, D_FF, D), D_FF ** -0.5),
        'moe_b2': nrm(ks[28], (DEPTH, N_EXPERTS, D), 0.02),
        'rel_bias': nrm(ks[29], (REL_BUCKETS, C_HEADS), 0.5),
        'final_g': 1.0 + nrm(ks[30], (D,), 0.02),
    }


def reference(x_prompt, x_sample, c_prompt, c_sample, ada_w, ada_b, norm1_g, w_in,
              gmlp_ln_g, gmlp_ln_b, gmlp_ws, gmlp_bs, mla_q_norm_g, mla_w_uq, mla_kv_norm_g,
              mla_w_ukv, diff_lq1, diff_lk1, diff_lq2, diff_lk2, diff_subln_g, w_out, norm2_g,
              router_w, router_b, moe_w1, moe_b1, moe_w2, moe_b2, rel_bias, final_g):
    y_prompt = trunk(x_prompt, c_prompt, ada_w, ada_b, norm1_g, w_in, gmlp_ln_g, gmlp_ln_b,
                     gmlp_ws, gmlp_bs, mla_q_norm_g, mla_w_uq, mla_kv_norm_g, mla_w_ukv,
                     diff_lq1, diff_lk1, diff_lq2, diff_lk2, diff_subln_g, w_out, norm2_g,
                     router_w, router_b, moe_w1, moe_b1, moe_w2, moe_b2, rel_bias, final_g)
    y_sample = trunk(x_sample, c_sample, ada_w, ada_b, norm1_g, w_in, gmlp_ln_g, gmlp_ln_b,
                     gmlp_ws, gmlp_bs, mla_q_norm_g, mla_w_uq, mla_kv_norm_g, mla_w_ukv,
                     diff_lq1, diff_lk1, diff_lq2, diff_lk2, diff_subln_g, w_out, norm2_g,
                     router_w, router_b, moe_w1, moe_b1, moe_w2, moe_b2, rel_bias, final_g)
    return (y_prompt, y_sample)
```

```python
import functools
import math

import jax
import jax.numpy as jnp
from jax import lax
from jax.experimental import pallas as pl
from jax.experimental.pallas import tpu as pltpu

F32 = jnp.float32
BF16 = jnp.bfloat16
HIGHEST = lax.Precision.HIGHEST
LOG2E = 1.4426950408889634

D_MODEL = 1024
EPS = 1e-6
GMLP_CHUNK = 128
A_GROUPS = 4
A_WIDTH = 256
B_HEADS = 6
B_Q_LORA = 384
B_KV_LORA = 256
B_NOPE = 64
B_ROPE = 32
B_V = 64
ROPE_BASE = 10000.0
C_HEADS = 6
C_QK = 32
C_V = 64
REL_BUCKETS = 32
REL_MAX_DIST = 128
N_EXPERTS = 32
TOP_K = 4
D_FF = 1024
SWIGLU_LIMIT = 7.0
SWIGLU_ALPHA = 1.702

LANES = 128
HEAD_PAD = 128
ONES_COL = 64
PV_ROWS = 80
NEG_BIG = -3.0e38
VMEM_LIMIT = 56 * 1024 * 1024

SEG_A = (0, 512)
SEG_CQ = (512, 896)
SEG_CKV = (896, 1152)
SEG_KR = (1152, 1280)
SEG_KROT = (1280, 1408)
SEG_DQ = (1408, 2176)
SEG_DK = (2176, 2944)
SEG_DV = (2944, 3712)
IN_PACKED = 3712


def _tiles(seq):
    return dict(
        tm=min(512, seq),
        tq=min(256, seq),
        tk=min(256, seq),
        vchunk=min(512, seq),
        td=min(128, seq),
        tmo=256,
        tc=min(128, seq),
    )


def _rms(x, g):
    return x * lax.rsqrt(jnp.mean(x * x, -1, keepdims=True) + EPS) * g


def _adaln_kernel(c_ref, w_ref, b_ref, o_ref):
    c = c_ref[...]
    cs = c * jax.nn.sigmoid(c)
    o_ref[0] = jnp.dot(cs, w_ref[0], precision=HIGHEST, preferred_element_type=F32) + b_ref[0]


def _adaln(c_pad, ada_w, ada_b):
    depth = ada_w.shape[0]
    nb = 6
    return pl.pallas_call(
        _adaln_kernel,
        grid=(depth, nb),
        in_specs=[
            pl.BlockSpec((8, D_MODEL), lambda l, j: (0, 0)),
            pl.BlockSpec((1, D_MODEL, D_MODEL), lambda l, j: (l, 0, j)),
            pl.BlockSpec((1, 1, D_MODEL), lambda l, j: (l, 0, j)),
        ],
        out_specs=pl.BlockSpec((1, 8, D_MODEL), lambda l, j: (l, 0, j)),
        out_shape=jax.ShapeDtypeStruct((depth, 8, 6 * D_MODEL), F32),
        compiler_params=pltpu.CompilerParams(dimension_semantics=("arbitrary", "arbitrary")),
        name="adaln",
    )(c_pad, ada_w, ada_b.reshape(depth, 1, 6 * D_MODEL))


def _pre_kernel(x_ref, mod_ref, n1g_ref, win_ref, lng_ref, lnb_ref, ws_ref, bs_ref,
                qng_ref, wqa_ref, wqb_ref, kvng_ref, wkn_ref, wv_ref,
                cosq_ref, sinq_ref, cosk_ref, sink_ref,
                mixa_ref, qb_ref, kb_ref, vb_ref, qd_ref, kd_ref, vd_ref, *, tm):
    x = x_ref[0]
    mod = mod_ref[0]
    h = (_rms(x, n1g_ref[...]) * (1.0 + mod[1:2]) + mod[0:1]).astype(BF16)

    def seg(s):
        return jnp.dot(h, win_ref[:, s[0]:s[1]], preferred_element_type=F32)

    za = seg(SEG_A)
    u = jax.nn.gelu(za[:, 0:A_WIDTH])
    v = jax.nn.gelu(za[:, A_WIDTH:2 * A_WIDTH])
    vc = v - jnp.mean(v, -1, keepdims=True)
    v = vc * lax.rsqrt(jnp.mean(vc * vc, -1, keepdims=True) + EPS) * lng_ref[...] + lnb_ref[...]
    v16 = v.astype(BF16)
    grp = lax.broadcasted_iota(jnp.int32, (1, A_WIDTH), 1) // (A_WIDTH // A_GROUPS)
    for n in range(tm // GMLP_CHUNK):
        rows = slice(n * GMLP_CHUNK, (n + 1) * GMLP_CHUNK)
        r = jnp.dot(ws_ref[...], v16[rows], preferred_element_type=F32)
        sv = bs_ref[...]
        for g in range(A_GROUPS):
            sv = sv + jnp.where(grp == g, r[g * GMLP_CHUNK:(g + 1) * GMLP_CHUNK], 0.0)
        mixa_ref[0, rows, :] = (u[rows] * sv).astype(BF16)

    lane = lax.broadcasted_iota(jnp.int32, (1, HEAD_PAD), 1)
    ones_col = jnp.where(lane == ONES_COL, 1.0, 0.0).astype(F32)

    cqn = _rms(seg(SEG_CQ), qng_ref[...]).astype(BF16)
    qa = jnp.dot(cqn, wqa_ref[...], preferred_element_type=F32)
    qb = jnp.dot(cqn, wqb_ref[...], preferred_element_type=F32)
    cosq = cosq_ref[...]
    sinq = sinq_ref[...]
    for hd in range(B_HEADS):
        cols = slice(hd * HEAD_PAD, (hd + 1) * HEAD_PAD)
        qb_ref[0, hd] = (qa[:, cols] * cosq + qb[:, cols] * sinq).astype(BF16)
    ckvn = _rms(seg(SEG_CKV), kvng_ref[...]).astype(BF16)
    kn = jnp.dot(ckvn, wkn_ref[...], preferred_element_type=F32)
    vv = jnp.dot(ckvn, wv_ref[...], preferred_element_type=F32)
    krp = seg(SEG_KR) * cosk_ref[...] + seg(SEG_KROT) * sink_ref[...]
    for hd in range(B_HEADS):
        cols = slice(hd * HEAD_PAD, (hd + 1) * HEAD_PAD)
        kb_ref[0, hd] = (kn[:, cols] + krp).astype(BF16)
        vb_ref[0, hd] = (vv[:, cols] + ones_col).astype(BF16)

    zdq = seg(SEG_DQ) * (C_QK ** -0.5 * LOG2E)
    zdk = seg(SEG_DK)
    zdv = seg(SEG_DV)
    for hd in range(C_HEADS):
        cols = slice(hd * HEAD_PAD, (hd + 1) * HEAD_PAD)
        qd_ref[0, hd] = zdq[:, cols].astype(BF16)
        kd_ref[0, hd] = zdk[:, cols].astype(BF16)
        vd_ref[0, hd] = (zdv[:, cols] + ones_col).astype(BF16)


def _pre(x, mod_l, lw, tabs, t):
    nb, seq, _ = x.shape
    tm = t["tm"]
    full = lambda a: pl.BlockSpec(a.shape, lambda b, i: (0,) * a.ndim)
    tab = pl.BlockSpec((tm, HEAD_PAD), lambda b, i: (i, 0))
    head_out = pl.BlockSpec((1, B_HEADS, tm, HEAD_PAD), lambda b, i: (b, 0, i, 0))
    head_shape = jax.ShapeDtypeStruct((nb, B_HEADS, seq, HEAD_PAD), BF16)
    consts = [lw["n1g"], lw["win"], lw["lng"], lw["lnb"], lw["ws"], lw["bs"],
              lw["qng"], lw["wqa"], lw["wqb"], lw["kvng"], lw["wkn"], lw["wv"]]
    return pl.pallas_call(
        functools.partial(_pre_kernel, tm=tm),
        grid=(nb, seq // tm),
        in_specs=[pl.BlockSpec((1, tm, D_MODEL), lambda b, i: (b, i, 0)),
                  pl.BlockSpec((1, 6, D_MODEL), lambda b, i: (b, 0, 0))]
                 + [full(a) for a in consts] + [tab] * 4,
        out_specs=[pl.BlockSpec((1, tm, A_WIDTH), lambda b, i: (b, i, 0))] + [head_out] * 6,
        out_shape=[jax.ShapeDtypeStruct((nb, seq, A_WIDTH), BF16)] + [head_shape] * 6,
        compiler_params=pltpu.CompilerParams(dimension_semantics=("arbitrary", "arbitrary"),
                                             vmem_limit_bytes=VMEM_LIMIT),
        name="pre",
    )(x, mod_l, *consts, tabs["cosq"], tabs["sinq"], tabs["cosk"], tabs["sink"])


def _transpose_v(v_ref, vt_ref, *, seq, vchunk, tk):
    per = vchunk // tk

    def body(i, carry):
        blk = v_ref[0, 0, pl.ds(pl.multiple_of(i * vchunk, vchunk), vchunk), :]
        blk_t = blk.astype(F32).T.astype(BF16)
        for c in range(per):
            vt_ref[i * per + c] = blk_t[:, c * tk:(c + 1) * tk]
        return carry

    lax.fori_loop(0, seq // vchunk, body, 0)


def _softmax_step(s, shift_extra, vt, m_ref, acc_ref):
    m_old = m_ref[...]
    m_new = jnp.maximum(m_old, jnp.max(s, axis=0, keepdims=True) + shift_extra)
    p = jnp.exp2(s - (m_new - shift_extra))
    alpha = jnp.exp2(m_old - m_new)
    pv = jnp.dot(vt[0:PV_ROWS], p.astype(BF16), preferred_element_type=F32)
    acc_ref[...] = acc_ref[...] * alpha + pv
    m_ref[...] = m_new


def _mla_attn_kernel(q_ref, k_ref, v_ref, o_ref, vt_ref, qt_ref, m_ref, acc_ref,
                     *, seq, tq, tk, vchunk):
    @pl.when(pl.program_id(2) == 0)
    def _():
        _transpose_v(v_ref, vt_ref, seq=seq, vchunk=vchunk, tk=tk)

    qt_ref[...] = q_ref[0, 0].astype(F32).T.astype(BF16)
    m_ref[...] = jnp.full(m_ref.shape, -jnp.inf, F32)
    acc_ref[...] = jnp.zeros(acc_ref.shape, F32)

    def kstep(j, carry):
        kt = k_ref[0, 0, pl.ds(pl.multiple_of(j * tk, tk), tk), :]
        s = jnp.dot(kt, qt_ref[...], preferred_element_type=F32)
        _softmax_step(s, 0.0, vt_ref[j], m_ref, acc_ref)
        return carry

    lax.fori_loop(0, seq // tk, kstep, 0)
    acc = acc_ref[...]
    o = acc[0:B_V] / acc[ONES_COL:ONES_COL + 1]
    o_ref[0, 0] = o.T.astype(BF16)


def _diff_attn_kernel(q_ref, k_ref, v_ref, bias_ref, cfar_ref, lamp_ref, g_ref, o_ref,
                      vt_ref, qt_ref, m_ref, acc_ref, *, seq, tq, tk, vchunk, n_band, lam_init):
    hd = pl.program_id(1)
    qi = pl.program_id(2)

    @pl.when(qi == 0)
    def _():
        _transpose_v(v_ref, vt_ref, seq=seq, vchunk=vchunk, tk=tk)

    qt = q_ref[0, 0].astype(F32).T
    row = lax.broadcasted_iota(jnp.int32, (HEAD_PAD, 1), 0)
    qt_ref[:, 0:tq] = jnp.where(row < C_QK, qt, 0.0).astype(BF16)
    qt_ref[:, tq:2 * tq] = jnp.where((row >= C_QK) & (row < 2 * C_QK), qt, 0.0).astype(BF16)
    m_ref[...] = jnp.full(m_ref.shape, -jnp.inf, F32)
    acc_ref[...] = jnp.zeros(acc_ref.shape, F32)

    nk = seq // tk
    j_lo = qi * (tq // tk) - 1

    def scores(j):
        kt = k_ref[0, 0, pl.ds(pl.multiple_of(j * tk, tk), tk), :]
        return jnp.dot(kt, qt_ref[...], preferred_element_type=F32)

    def far_step(c):
        def step(j, carry):
            _softmax_step(scores(j), c, vt_ref[j], m_ref, acc_ref)
            return carry
        return step

    lax.fori_loop(0, jnp.maximum(j_lo, 0), far_step(cfar_ref[hd, 0]), 0)
    for i in range(n_band):
        j = j_lo + i

        @pl.when((j >= 0) & (j < nk))
        def _():
            b = bias_ref[0, i]
            s = scores(j) + jnp.concatenate([b, b], axis=1)
            _softmax_step(s, 0.0, vt_ref[j], m_ref, acc_ref)
    lax.fori_loop(jnp.minimum(j_lo + n_band, nk), nk, far_step(cfar_ref[hd, 1]), 0)

    lp = lamp_ref[...]
    lam = (jnp.exp(jnp.sum(lp[0:1] * lp[1:2], keepdims=True))
           - jnp.exp(jnp.sum(lp[2:3] * lp[3:4], keepdims=True)) + lam_init)
    acc = acc_ref[...]
    o1 = acc[0:C_V, 0:tq] / acc[ONES_COL:ONES_COL + 1, 0:tq]
    o2 = acc[0:C_V, tq:2 * tq] / acc[ONES_COL:ONES_COL + 1, tq:2 * tq]
    o = (o1 - lam * o2).T
    o_ref[0, 0] = (_rms(o, g_ref[...]) * (1.0 - lam_init)).astype(BF16)


def _attn_common(nb, seq, t, n_cols):
    tq, tk = t["tq"], t["tk"]
    grid = (nb, B_HEADS, seq // tq)
    q_spec = pl.BlockSpec((1, 1, tq, HEAD_PAD), lambda b, h, i: (b, h, i, 0))
    kv_spec = pl.BlockSpec((1, 1, seq, HEAD_PAD), lambda b, h, i: (b, h, 0, 0))
    out_spec = pl.BlockSpec((1, 1, tq, B_V), lambda b, h, i: (b, h, i, 0))
    out_shape = jax.ShapeDtypeStruct((nb, B_HEADS, seq, B_V), BF16)
    scratch = [pltpu.VMEM((seq // tk, HEAD_PAD, tk), BF16),
               pltpu.VMEM((HEAD_PAD, n_cols), BF16),
               pltpu.VMEM((1, n_cols), F32),
               pltpu.VMEM((PV_ROWS, n_cols), F32)]
    params = pltpu.CompilerParams(dimension_semantics=("arbitrary",) * 3,
                                  vmem_limit_bytes=VMEM_LIMIT)
    return grid, q_spec, kv_spec, out_spec, out_shape, scratch, params


def _mla_attn(q, k, v, t):
    nb, _, seq, _ = q.shape
    grid, q_spec, kv_spec, out_spec, out_shape, scratch, params = _attn_common(nb, seq, t, t["tq"])
    return pl.pallas_call(
        functools.partial(_mla_attn_kernel, seq=seq, tq=t["tq"], tk=t["tk"], vchunk=t["vchunk"]),
        grid=grid, in_specs=[q_spec, kv_spec, kv_spec], out_specs=out_spec, out_shape=out_shape,
        scratch_shapes=scratch, compiler_params=params, name="mla_attn",
    )(q, k, v)


def _diff_attn(q, k, v, bias_tiles, cfar, lam_params, subln_g, lam_init, t):
    nb, _, seq, _ = q.shape
    tq, tk = t["tq"], t["tk"]
    n_band = bias_tiles.shape[1]
    grid, q_spec, kv_spec, out_spec, out_shape, scratch, params = _attn_common(nb, seq, t, 2 * tq)
    return pl.pallas_call(
        functools.partial(_diff_attn_kernel, seq=seq, tq=tq, tk=tk, vchunk=t["vchunk"],
                          n_band=n_band, lam_init=lam_init),
        grid=grid,
        in_specs=[q_spec, kv_spec, kv_spec,
                  pl.BlockSpec((1, n_band, tk, tq), lambda b, h, i: (h, 0, 0, 0)),
                  pl.BlockSpec(memory_space=pltpu.SMEM),
                  pl.BlockSpec((8, LANES), lambda b, h, i: (0, 0)),
                  pl.BlockSpec((1, C_V), lambda b, h, i: (0, 0))],
        out_specs=out_spec, out_shape=out_shape, scratch_shapes=scratch, compiler_params=params,
        name="diff_attn",
    )(q, k, v, bias_tiles, cfar, lam_params, subln_g)


def _bias_kernel(bucket_ref, rb_ref, o_ref):
    hd = pl.program_id(0)
    bucket = bucket_ref[0]
    acc = jnp.zeros(bucket.shape, F32)
    for b in range(REL_BUCKETS):
        acc = jnp.where(bucket == b, rb_ref[b, hd], acc)
    o_ref[0, 0] = acc * LOG2E


def _bias_tiles(bucket_t, rel_bias):
    n_band, tk, tq = bucket_t.shape
    return pl.pallas_call(
        _bias_kernel,
        grid=(C_HEADS, n_band),
        in_specs=[pl.BlockSpec((1, tk, tq), lambda h, i: (i, 0, 0)),
                  pl.BlockSpec(memory_space=pltpu.SMEM)],
        out_specs=pl.BlockSpec((1, 1, tk, tq), lambda h, i: (h, i, 0, 0)),
        out_shape=jax.ShapeDtypeStruct((C_HEADS, n_band, tk, tq), F32),
        name="bias_tiles",
    )(bucket_t, rel_bias)


def _post_kernel(x_ref, mixa_ref, ob_ref, oc_ref, wout_ref, mod_ref, n2g_ref, rw_ref, rb_ref,
                 x2_ref, h2_ref, idx_ref, gate_ref, rank_ref, cnt_ref, mix_scr, base_scr, *, tm):
    @pl.when((pl.program_id(0) == 0) & (pl.program_id(1) == 0))
    def _():
        base_scr[...] = jnp.zeros(base_scr.shape, F32)

    mix_scr[:, 0:A_WIDTH] = mixa_ref[0]
    for hd in range(B_HEADS):
        mix_scr[:, A_WIDTH + hd * B_V:A_WIDTH + (hd + 1) * B_V] = ob_ref[0, hd]
    off_c = A_WIDTH + B_HEADS * B_V
    for hd in range(C_HEADS):
        mix_scr[:, off_c + hd * C_V:off_c + (hd + 1) * C_V] = oc_ref[0, hd]
    y = jnp.dot(mix_scr[...], wout_ref[...], preferred_element_type=F32)
    mod = mod_ref[0]
    x2 = x_ref[0] + mod[2:3] * y
    x2_ref[0] = x2
    h2 = _rms(x2, n2g_ref[...]) * (1.0 + mod[4:5]) + mod[3:4]
    h2_ref[0] = h2

    logits = jnp.dot(h2, rw_ref[...], precision=HIGHEST, preferred_element_type=F32) + rb_ref[...]
    lane = lax.broadcasted_iota(jnp.int32, (tm, LANES), 1)
    vals, idxs = [], []
    cur = logits
    for _ in range(TOP_K):
        mx = jnp.max(cur, axis=1, keepdims=True)
        ik = jnp.min(jnp.where(cur == mx, lane, LANES), axis=1, keepdims=True)
        vals.append(mx)
        idxs.append(ik)
        cur = jnp.where(lane == ik, NEG_BIG, cur)
    ex = [jnp.exp(vk - vals[0]) for vk in vals]
    den = ex[0] + ex[1] + ex[2] + ex[3]

    onehots = [(lane == ik).astype(F32) for ik in idxs]
    esum = onehots[0] + onehots[1] + onehots[2] + onehots[3]
    r_i = lax.broadcasted_iota(jnp.int32, (tm, tm), 0)
    c_i = lax.broadcasted_iota(jnp.int32, (tm, tm), 1)
    lower = jnp.where(r_i > c_i, 1.0, 0.0).astype(BF16)
    before = base_scr[...] + jnp.dot(lower, esum.astype(BF16), preferred_element_type=F32)
    idx_out = jnp.zeros((tm, LANES), jnp.int32)
    gate_out = jnp.zeros((tm, LANES), F32)
    rank_out = jnp.zeros((tm, LANES), jnp.int32)
    for k in range(TOP_K):
        rk = jnp.sum(onehots[k] * before, axis=1, keepdims=True).astype(jnp.int32)
        idx_out = jnp.where(lane == k, idxs[k], idx_out)
        gate_out = jnp.where(lane == k, ex[k] / den, gate_out)
        rank_out = jnp.where(lane == k, rk, rank_out)
    idx_ref[0] = idx_out
    gate_ref[0] = gate_out
    rank_ref[0] = rank_out
    base_new = base_scr[...] + jnp.sum(esum, axis=0, keepdims=True)
    base_scr[...] = base_new
    cnt_ref[...] = base_new


def _post(x, mixa, ob, oc, mod_l, lw, t):
    nb, seq, _ = x.shape
    tm = t["tm"]
    full = lambda a: pl.BlockSpec(a.shape, lambda b, i: (0,) * a.ndim)
    tok = lambda w: pl.BlockSpec((1, tm, w), lambda b, i: (b, i, 0))
    head_in = pl.BlockSpec((1, B_HEADS, tm, B_V), lambda b, i: (b, 0, i, 0))
    return pl.pallas_call(
        functools.partial(_post_kernel, tm=tm),
        grid=(nb, seq // tm),
        in_specs=[tok(D_MODEL), tok(A_WIDTH), head_in, head_in, full(lw["wout"]),
                  pl.BlockSpec((1, 6, D_MODEL), lambda b, i: (b, 0, 0)),
                  full(lw["n2g"]), full(lw["rw"]), full(lw["rb"])],
        out_specs=[tok(D_MODEL), tok(D_MODEL), tok(LANES), tok(LANES), tok(LANES),
                   pl.BlockSpec((1, LANES), lambda b, i: (0, 0))],
        out_shape=[jax.ShapeDtypeStruct((nb, seq, D_MODEL), F32),
                   jax.ShapeDtypeStruct((nb, seq, D_MODEL), F32),
                   jax.ShapeDtypeStruct((nb, seq, LANES), jnp.int32),
                   jax.ShapeDtypeStruct((nb, seq, LANES), F32),
                   jax.ShapeDtypeStruct((nb, seq, LANES), jnp.int32),
                   jax.ShapeDtypeStruct((1, LANES), F32)],
        scratch_shapes=[pltpu.VMEM((tm, D_MODEL), BF16), pltpu.VMEM((1, LANES), F32)],
        compiler_params=pltpu.CompilerParams(dimension_semantics=("arbitrary", "arbitrary"),
                                             vmem_limit_bytes=VMEM_LIMIT),
        name="post",
    )(x, mixa, ob, oc, lw["wout"], mod_l, lw["n2g"], lw["rw"], lw["rb"])


def _row_copy(src, src_row, dst, dst_row, sem):
    return pltpu.make_async_copy(src.at[pl.ds(src_row, 1)], dst.at[pl.ds(dst_row, 1)], sem)


def _dispatch_kernel(pos_ref, h2_ref, xs_in_ref, xs_ref, sem, *, td):
    del xs_in_ref

    def start(r, carry):
        for k in range(TOP_K):
            _row_copy(h2_ref, r, xs_ref, pos_ref[0, 0, r * TOP_K + k], sem).start()
        return carry

    def wait(r, carry):
        for k in range(TOP_K):
            _row_copy(h2_ref, 0, xs_ref, 0, sem).wait()
        return carry

    lax.fori_loop(0, td, start, 0)
    lax.fori_loop(0, td, wait, 0)


def _dispatch(pos, h2, n_slots, t):
    n_tok = h2.shape[0]
    td = t["td"]
    xs0 = jnp.zeros((n_slots, D_MODEL), F32)
    return pl.pallas_call(
        functools.partial(_dispatch_kernel, td=td),
        grid=(n_tok // td,),
        in_specs=[pl.BlockSpec((1, 1, td * TOP_K), lambda i: (i, 0, 0), memory_space=pltpu.SMEM),
                  pl.BlockSpec((td, D_MODEL), lambda i: (i, 0)),
                  pl.BlockSpec(memory_space=pl.ANY)],
        out_specs=pl.BlockSpec(memory_space=pl.ANY),
        out_shape=jax.ShapeDtypeStruct((n_slots, D_MODEL), F32),
        scratch_shapes=[pltpu.SemaphoreType.DMA(())],
        input_output_aliases={2: 0},
        compiler_params=pltpu.CompilerParams(dimension_semantics=("arbitrary",),
                                             has_side_effects=True),
        name="dispatch",
    )(pos.reshape(n_tok // td, 1, td * TOP_K), h2, xs0)


def _expert_kernel(blk_e_ref, n_used_ref, xs_ref, w1_ref, b1_ref, w2_ref, b2_ref, y_ref):
    del blk_e_ref
    i = pl.program_id(0)

    @pl.when(i < n_used_ref[0])
    def _():
        hmid = jnp.dot(xs_ref[...].astype(BF16), w1_ref[0], preferred_element_type=F32) + b1_ref[0]
        gate = jnp.minimum(hmid[:, 0:D_FF], SWIGLU_LIMIT)
        up = jnp.clip(hmid[:, D_FF:2 * D_FF], -SWIGLU_LIMIT, SWIGLU_LIMIT)
        act = (up + 1.0) * (gate * jax.nn.sigmoid(SWIGLU_ALPHA * gate))
        y_ref[...] = jnp.dot(act.astype(BF16), w2_ref[0], preferred_element_type=F32) + b2_ref[0]

    @pl.when(i >= n_used_ref[0])
    def _():
        y_ref[...] = jnp.zeros(y_ref.shape, F32)


def _experts(blk_e, n_used, xs, lw, t):
    n_slots = xs.shape[0]
    tmo = t["tmo"]
    grid_spec = pltpu.PrefetchScalarGridSpec(
        num_scalar_prefetch=2,
        grid=(n_slots // tmo,),
        in_specs=[pl.BlockSpec((tmo, D_MODEL), lambda i, e, n: (i, 0)),
                  pl.BlockSpec((1, D_MODEL, 2 * D_FF), lambda i, e, n: (e[i], 0, 0)),
                  pl.BlockSpec((1, 1, 2 * D_FF), lambda i, e, n: (e[i], 0, 0)),
                  pl.BlockSpec((1, D_FF, D_MODEL), lambda i, e, n: (e[i], 0, 0)),
                  pl.BlockSpec((1, 1, D_MODEL), lambda i, e, n: (e[i], 0, 0))],
        out_specs=pl.BlockSpec((tmo, D_MODEL), lambda i, e, n: (i, 0)),
    )
    return pl.pallas_call(
        _expert_kernel,
        grid_spec=grid_spec,
        out_shape=jax.ShapeDtypeStruct((n_slots, D_MODEL), F32),
        compiler_params=pltpu.CompilerParams(dimension_semantics=("arbitrary",),
                                             vmem_limit_bytes=VMEM_LIMIT),
        name="experts",
    )(blk_e, n_used, xs, lw["w1"], lw["b1"], lw["w2"], lw["b2"])


def _combine_kernel(pos_ref, gate_ref, x2_ref, mod_ref, fg_ref, y_ref, o_ref, buf, sem, *, tc, final):
    def start(r, carry):
        for k in range(TOP_K):
            _row_copy(y_ref, pos_ref[0, 0, r * TOP_K + k], buf.at[k], r, sem).start()
        return carry

    def wait(r, carry):
        for k in range(TOP_K):
            _row_copy(y_ref, 0, buf.at[k], 0, sem).wait()
        return carry

    lax.fori_loop(0, tc, start, 0)
    lax.fori_loop(0, tc, wait, 0)
    gates = gate_ref[...]
    moe = buf[0] * gates[:, 0:1]
    for k in range(1, TOP_K):
        moe = moe + buf[k] * gates[:, k:k + 1]
    out = x2_ref[...] + mod_ref[0][5:6] * moe
    if final:
        out = _rms(out, fg_ref[...])
    o_ref[...] = out


def _combine(pos, gates, x2, mod_l, final_g, y, seq, t, final):
    n_tok = x2.shape[0]
    tc = t["tc"]
    per_b = seq // tc
    return pl.pallas_call(
        functools.partial(_combine_kernel, tc=tc, final=final),
        grid=(n_tok // tc,),
        in_specs=[pl.BlockSpec((1, 1, tc * TOP_K), lambda i: (i, 0, 0), memory_space=pltpu.SMEM),
                  pl.BlockSpec((tc, LANES), lambda i: (i, 0)),
                  pl.BlockSpec((tc, D_MODEL), lambda i: (i, 0)),
                  pl.BlockSpec((1, 6, D_MODEL), lambda i: (i // per_b, 0, 0)),
                  pl.BlockSpec((1, D_MODEL), lambda i: (0, 0)),
                  pl.BlockSpec(memory_space=pl.ANY)],
        out_specs=pl.BlockSpec((tc, D_MODEL), lambda i: (i, 0)),
        out_shape=jax.ShapeDtypeStruct((n_tok, D_MODEL), F32),
        scratch_shapes=[pltpu.VMEM((TOP_K, tc, D_MODEL), F32), pltpu.SemaphoreType.DMA(())],
        compiler_params=pltpu.CompilerParams(dimension_semantics=("arbitrary",),
                                             vmem_limit_bytes=VMEM_LIMIT),
        name="combine",
    )(pos.reshape(n_tok // tc, 1, tc * TOP_K), gates, x2, mod_l, final_g, y)


def _rot_half_cols(w):
    half = w.shape[-1] // 2
    return jnp.concatenate([-w[..., half:], w[..., :half]], -1)


def _pad_heads(w, n_heads, width):
    lead = w.shape[:-1]
    w = w.reshape(lead + (n_heads, width))
    w = jnp.pad(w, [(0, 0)] * len(lead) + [(0, 0), (0, HEAD_PAD - width)])
    return w.reshape(lead + (n_heads * HEAD_PAD,))


def _rope_block(w):
    return jnp.pad(w, [(0, 0)] * (w.ndim - 1) + [(B_NOPE, HEAD_PAD - B_NOPE - B_ROPE)])


def _pack_layer_weights(w_in, mla_w_uq, mla_w_ukv):
    o = [0, 256, 512, 896, 1152, 1184, 1568, 1952, 2336]
    a = w_in[..., o[0]:o[2]]
    cq = w_in[..., o[2]:o[3]]
    ckv = w_in[..., o[3]:o[4]]
    kr = w_in[..., o[4]:o[5]]
    dq, dk, dv = (w_in[..., o[5]:o[6]], w_in[..., o[6]:o[7]], w_in[..., o[7]:o[8]])
    win = jnp.concatenate([a, cq, ckv, _rope_block(kr), _rope_block(_rot_half_cols(kr)),
                           _pad_heads(dq, C_HEADS, 2 * C_QK), _pad_heads(dk, C_HEADS, 2 * C_QK),
                           _pad_heads(dv, C_HEADS, C_V)], -1).astype(BF16)
    lead = mla_w_uq.shape[:-1]
    uq = mla_w_uq.reshape(lead + (B_HEADS, B_NOPE + B_ROPE))
    nope, rp = uq[..., :B_NOPE], uq[..., B_NOPE:]
    zeros32 = jnp.zeros_like(rp)
    wqa = jnp.concatenate([nope, rp, zeros32], -1).reshape(lead + (B_HEADS * HEAD_PAD,))
    wqb = jnp.concatenate([jnp.zeros_like(nope), _rot_half_cols(rp), zeros32], -1)
    wqb = wqb.reshape(lead + (B_HEADS * HEAD_PAD,))
    lead = mla_w_ukv.shape[:-1]
    ukv = mla_w_ukv.reshape(lead + (B_HEADS, B_NOPE + B_V))
    kn, vv = ukv[..., :B_NOPE], ukv[..., B_NOPE:]
    wkn = jnp.concatenate([kn, jnp.zeros_like(kn)], -1).reshape(lead + (B_HEADS * HEAD_PAD,))
    wv = jnp.concatenate([vv, jnp.zeros_like(vv)], -1).reshape(lead + (B_HEADS * HEAD_PAD,))
    return win, wqa.astype(BF16), wqb.astype(BF16), wkn.astype(BF16), wv.astype(BF16)


def _rope_tables(seq):
    pos = jnp.arange(seq, dtype=F32)
    inv = 1.0 / (ROPE_BASE ** (jnp.arange(B_ROPE // 2, dtype=F32) / (B_ROPE // 2)))
    ang = pos[:, None] * inv[None, :]
    cos = jnp.concatenate([jnp.cos(ang)] * 2, -1)
    sin = jnp.concatenate([jnp.sin(ang)] * 2, -1)
    qscale = (B_NOPE + B_ROPE) ** -0.5 * LOG2E
    ones = jnp.ones((seq, B_NOPE), F32)
    pad = jnp.zeros((seq, HEAD_PAD - B_NOPE - B_ROPE), F32)
    return dict(cosq=jnp.concatenate([ones, cos, pad], -1) * qscale,
                sinq=_rope_block(sin) * qscale,
                cosk=_rope_block(cos), sink=_rope_block(sin))


def _t5_bucket(rel):
    half = REL_BUCKETS // 2
    max_exact = half // 2
    base = jnp.where(rel > 0, half, 0)
    n = jnp.abs(rel)
    nf = jnp.maximum(n, 1).astype(F32)
    large = max_exact + (jnp.log(nf / max_exact) / math.log(REL_MAX_DIST / max_exact)
                         * (half - max_exact)).astype(jnp.int32)
    large = jnp.minimum(large, half - 1)
    return base + jnp.where(n < max_exact, n, large)


def _band_buckets(tq, tk):
    n_band = tq // tk + 2
    kk = jnp.arange(tk)[:, None]
    qq = jnp.arange(tq)[None, :]
    return jnp.stack([_t5_bucket((i - 1) * tk + kk - qq) for i in range(n_band)], 0)


def kernel(x_prompt, x_sample, c_prompt, c_sample, ada_w, ada_b, norm1_g, w_in, gmlp_ln_g, gmlp_ln_b,
           gmlp_ws, gmlp_bs, mla_q_norm_g, mla_w_uq, mla_kv_norm_g, mla_w_ukv, diff_lq1, diff_lk1,
           diff_lq2, diff_lk2, diff_subln_g, w_out, norm2_g, router_w, router_b, moe_w1, moe_b1,
           moe_w2, moe_b2, rel_bias, final_g):
    depth = ada_w.shape[0]
    n_prompt = x_prompt.shape[0]
    assert x_prompt.shape[1] == x_sample.shape[1]
    x = jnp.concatenate([x_prompt, x_sample], 0)
    nb, seq, _ = x.shape
    n_tok = nb * seq
    t = _tiles(seq)
    assert t["tk"] >= REL_MAX_DIST and t["tq"] % t["tk"] == 0

    c = jnp.concatenate([c_prompt, c_sample], 0)
    c_pad = jnp.pad(c, ((0, 8 - nb), (0, 0)))
    mod = _adaln(c_pad, ada_w, ada_b)[:, :nb].reshape(depth, nb, 6, D_MODEL)

    win, wqa, wqb, wkn, wv = _pack_layer_weights(w_in, mla_w_uq, mla_w_ukv)
    ws_stack = gmlp_ws.reshape(depth, A_GROUPS * GMLP_CHUNK, GMLP_CHUNK).astype(BF16)
    bs_tile = jnp.repeat(jnp.swapaxes(gmlp_bs, 1, 2), A_WIDTH // A_GROUPS, axis=2)
    wout16 = w_out.astype(BF16)
    rw_pad = jnp.pad(router_w, ((0, 0), (0, 0), (0, LANES - N_EXPERTS)))
    rb_pad = jnp.pad(router_b, ((0, 0), (0, LANES - N_EXPERTS)), constant_values=NEG_BIG)
    w1_16 = moe_w1.astype(BF16)
    w2_16 = moe_w2.astype(BF16)
    tabs = _rope_tables(seq)
    bias_tiles = _bias_tiles(_band_buckets(t["tq"], t["tk"]).astype(jnp.int32), rel_bias)
    half = REL_BUCKETS // 2
    cfar = jnp.stack([rel_bias[half - 1], rel_bias[REL_BUCKETS - 1]], -1) * LOG2E
    lam_rows = jnp.stack([diff_lq1, diff_lk1, diff_lq2, diff_lk2], 1)
    lam_rows = jnp.pad(lam_rows, ((0, 0), (0, 4), (0, LANES - C_QK)))

    tmo = t["tmo"]
    n_tiles = -(-(n_tok * TOP_K) // tmo) + N_EXPERTS
    n_slots = n_tiles * tmo
    final_row = final_g.reshape(1, D_MODEL)

    for l in range(depth):
        lw = dict(n1g=norm1_g[l][None], win=win[l], lng=gmlp_ln_g[l][None], lnb=gmlp_ln_b[l][None],
                  ws=ws_stack[l], bs=bs_tile[l], qng=mla_q_norm_g[l][None], wqa=wqa[l], wqb=wqb[l],
                  kvng=mla_kv_norm_g[l][None], wkn=wkn[l], wv=wv[l], wout=wout16[l],
                  n2g=norm2_g[l][None], rw=rw_pad[l], rb=rb_pad[l][None],
                  w1=w1_16[l], b1=moe_b1[l][:, None, :], w2=w2_16[l], b2=moe_b2[l][:, None, :])
        lam_init = 0.8 - 0.6 * math.exp(-0.3 * l)
        mixa, qb, kb, vb, qd, kd, vd = _pre(x, mod[l], lw, tabs, t)
        ob = _mla_attn(qb, kb, vb, t)
        oc = _diff_attn(qd, kd, vd, bias_tiles, cfar, lam_rows[l], diff_subln_g[l][None], lam_init, t)
        x2, h2, idx, gates, rank, cnt = _post(x, mixa, ob, oc, mod[l], lw, t)

        counts = cnt[0, :N_EXPERTS].astype(jnp.int32)
        padded = ((counts + tmo - 1) // tmo) * tmo
        pend = jnp.cumsum(padded)
        pstart = pend - padded
        idx4 = idx.reshape(n_tok, LANES)[:, :TOP_K]
        pos = (jnp.take(pstart, idx4) + rank.reshape(n_tok, LANES)[:, :TOP_K]).reshape(-1)
        blk_e = jnp.minimum(jnp.searchsorted(pend, jnp.arange(n_tiles) * tmo, side="right"),
                            N_EXPERTS - 1).astype(jnp.int32)
        n_used = (pend[-1:] // tmo).astype(jnp.int32)

        xs = _dispatch(pos, h2.reshape(n_tok, D_MODEL), n_slots, t)
        y = _experts(blk_e, n_used, xs, lw, t)
        x = _combine(pos, gates.reshape(n_tok, LANES), x2.reshape(n_tok, D_MODEL), mod[l], final_row,
                     y, seq, t, final=(l == depth - 1)).reshape(nb, seq, D_MODEL)
    return x[:n_prompt], x[n_prompt:]
```

```python
import functools
import math

import jax
import jax.numpy as jnp
from jax import lax
from jax.experimental import pallas as pl
from jax.experimental.pallas import tpu as pltpu

F32 = jnp.float32
BF16 = jnp.bfloat16
HIGHEST = lax.Precision.HIGHEST
LOG2E = 1.4426950408889634

D_MODEL = 1024
EPS = 1e-6
GMLP_CHUNK = 128
A_GROUPS = 4
A_WIDTH = 256
B_HEADS = 6
B_Q_LORA = 384
B_KV_LORA = 256
B_NOPE = 64
B_ROPE = 32
B_V = 64
ROPE_BASE = 10000.0
C_HEADS = 6
C_QK = 32
C_V = 64
REL_BUCKETS = 32
REL_MAX_DIST = 128
N_EXPERTS = 32
TOP_K = 4
D_FF = 1024
SWIGLU_LIMIT = 7.0
SWIGLU_ALPHA = 1.702

LANES = 128
HEAD_PAD = 128
ONES_COL = 64
PV_ROWS = 80
NEG_BIG = -3.0e38
VMEM_LIMIT = 56 * 1024 * 1024

SEG_A = (0, 512)
SEG_CQ = (512, 896)
SEG_CKV = (896, 1152)
SEG_KR = (1152, 1280)
SEG_KROT = (1280, 1408)
SEG_DQ = (1408, 2176)
SEG_DK = (2176, 2944)
SEG_DV = (2944, 3712)
IN_PACKED = 3712


def _tiles(seq):
    return dict(
        tm=min(512, seq),
        tq_mla=min(2048, seq),
        tq_diff=min(1024, seq),
        tk=min(256, seq),
        vchunk=min(512, seq),
        td=min(128, seq),
        tmo=256,
        tc=min(128, seq),
    )


def _rms(x, g):
    return x * lax.rsqrt(jnp.mean(x * x, -1, keepdims=True) + EPS) * g


def _adaln_kernel(c_ref, w_ref, b_ref, o_ref):
    c = c_ref[...]
    cs = c * jax.nn.sigmoid(c)
    o_ref[0] = jnp.dot(cs, w_ref[0], precision=HIGHEST, preferred_element_type=F32) + b_ref[0]


def _adaln(c_pad, ada_w, ada_b):
    depth = ada_w.shape[0]
    nb = 6
    return pl.pallas_call(
        _adaln_kernel,
        grid=(depth, nb),
        in_specs=[
            pl.BlockSpec((8, D_MODEL), lambda l, j: (0, 0)),
            pl.BlockSpec((1, D_MODEL, D_MODEL), lambda l, j: (l, 0, j)),
            pl.BlockSpec((1, 1, D_MODEL), lambda l, j: (l, 0, j)),
        ],
        out_specs=pl.BlockSpec((1, 8, D_MODEL), lambda l, j: (l, 0, j)),
        out_shape=jax.ShapeDtypeStruct((depth, 8, 6 * D_MODEL), F32),
        compiler_params=pltpu.CompilerParams(dimension_semantics=("arbitrary", "arbitrary")),
        name="adaln",
    )(c_pad, ada_w, ada_b.reshape(depth, 1, 6 * D_MODEL))


def _pre_kernel(x_ref, mod_ref, n1g_ref, win_ref, lng_ref, lnb_ref, ws_ref, bs_ref,
                qng_ref, wqa_ref, wqb_ref, kvng_ref, wkn_ref, wv_ref,
                cosq_ref, sinq_ref, cosk_ref, sink_ref,
                mixa_ref, qb_ref, kb_ref, vb_ref, qd_ref, kd_ref, vd_ref, *, tm):
    x = x_ref[0]
    mod = mod_ref[0]
    h = (_rms(x, n1g_ref[...]) * (1.0 + mod[1:2]) + mod[0:1]).astype(BF16)

    def seg(s):
        return jnp.dot(h, win_ref[:, s[0]:s[1]], preferred_element_type=F32)

    za = seg(SEG_A)
    u = jax.nn.gelu(za[:, 0:A_WIDTH])
    v = jax.nn.gelu(za[:, A_WIDTH:2 * A_WIDTH])
    vc = v - jnp.mean(v, -1, keepdims=True)
    v = vc * lax.rsqrt(jnp.mean(vc * vc, -1, keepdims=True) + EPS) * lng_ref[...] + lnb_ref[...]
    v16 = v.astype(BF16)
    grp = lax.broadcasted_iota(jnp.int32, (1, A_WIDTH), 1) // (A_WIDTH // A_GROUPS)
    for n in range(tm // GMLP_CHUNK):
        rows = slice(n * GMLP_CHUNK, (n + 1) * GMLP_CHUNK)
        r = jnp.dot(ws_ref[...], v16[rows], preferred_element_type=F32)
        sv = bs_ref[...]
        for g in range(A_GROUPS):
            sv = sv + jnp.where(grp == g, r[g * GMLP_CHUNK:(g + 1) * GMLP_CHUNK], 0.0)
        mixa_ref[0, rows, :] = (u[rows] * sv).astype(BF16)

    lane = lax.broadcasted_iota(jnp.int32, (1, HEAD_PAD), 1)
    ones_col = jnp.where(lane == ONES_COL, 1.0, 0.0).astype(F32)

    cqn = _rms(seg(SEG_CQ), qng_ref[...]).astype(BF16)
    qa = jnp.dot(cqn, wqa_ref[...], preferred_element_type=F32)
    qb = jnp.dot(cqn, wqb_ref[...], preferred_element_type=F32)
    cosq = cosq_ref[...]
    sinq = sinq_ref[...]
    for hd in range(B_HEADS):
        cols = slice(hd * HEAD_PAD, (hd + 1) * HEAD_PAD)
        qb_ref[0, hd] = (qa[:, cols] * cosq + qb[:, cols] * sinq).astype(BF16)
    ckvn = _rms(seg(SEG_CKV), kvng_ref[...]).astype(BF16)
    kn = jnp.dot(ckvn, wkn_ref[...], preferred_element_type=F32)
    vv = jnp.dot(ckvn, wv_ref[...], preferred_element_type=F32)
    krp = seg(SEG_KR) * cosk_ref[...] + seg(SEG_KROT) * sink_ref[...]
    for hd in range(B_HEADS):
        cols = slice(hd * HEAD_PAD, (hd + 1) * HEAD_PAD)
        kb_ref[0, hd] = (kn[:, cols] + krp).astype(BF16)
        vb_ref[0, hd] = (vv[:, cols] + ones_col).astype(BF16)

    zdq = seg(SEG_DQ) * (C_QK ** -0.5 * LOG2E)
    zdk = seg(SEG_DK)
    zdv = seg(SEG_DV)
    for hd in range(C_HEADS):
        cols = slice(hd * HEAD_PAD, (hd + 1) * HEAD_PAD)
        qd_ref[0, hd] = zdq[:, cols].astype(BF16)
        kd_ref[0, hd] = zdk[:, cols].astype(BF16)
        vd_ref[0, hd] = (zdv[:, cols] + ones_col).astype(BF16)


def _pre(x, mod_l, lw, tabs, t):
    nb, seq, _ = x.shape
    tm = t["tm"]
    full = lambda a: pl.BlockSpec(a.shape, lambda b, i: (0,) * a.ndim)
    tab = pl.BlockSpec((tm, HEAD_PAD), lambda b, i: (i, 0))
    head_out = pl.BlockSpec((1, B_HEADS, tm, HEAD_PAD), lambda b, i: (b, 0, i, 0))
    head_shape = jax.ShapeDtypeStruct((nb, B_HEADS, seq, HEAD_PAD), BF16)
    consts = [lw["n1g"], lw["win"], lw["lng"], lw["lnb"], lw["ws"], lw["bs"],
              lw["qng"], lw["wqa"], lw["wqb"], lw["kvng"], lw["wkn"], lw["wv"]]
    return pl.pallas_call(
        functools.partial(_pre_kernel, tm=tm),
        grid=(nb, seq // tm),
        in_specs=[pl.BlockSpec((1, tm, D_MODEL), lambda b, i: (b, i, 0)),
                  pl.BlockSpec((1, 6, D_MODEL), lambda b, i: (b, 0, 0))]
                 + [full(a) for a in consts] + [tab] * 4,
        out_specs=[pl.BlockSpec((1, tm, A_WIDTH), lambda b, i: (b, i, 0))] + [head_out] * 6,
        out_shape=[jax.ShapeDtypeStruct((nb, seq, A_WIDTH), BF16)] + [head_shape] * 6,
        compiler_params=pltpu.CompilerParams(dimension_semantics=("arbitrary", "arbitrary"),
                                             vmem_limit_bytes=VMEM_LIMIT),
        name="pre",
    )(x, mod_l, *consts, tabs["cosq"], tabs["sinq"], tabs["cosk"], tabs["sink"])


def _transpose_v(v_ref, vt_ref, *, seq, vchunk, tk):
    per = vchunk // tk

    def body(i, carry):
        blk = v_ref[0, 0, pl.ds(pl.multiple_of(i * vchunk, vchunk), vchunk), :]
        blk_t = blk.astype(F32).T.astype(BF16)
        for c in range(per):
            vt_ref[i * per + c] = blk_t[:, c * tk:(c + 1) * tk]
        return carry

    lax.fori_loop(0, seq // vchunk, body, 0)


def _attn_pipeline(k_ref, vt_ref, qt_ref, bias_fn, scr, *, nk, tk):
    m_ref, acc_ref, s_buf, cm_buf, p_buf, a_buf = scr
    m_ref[...] = jnp.full(m_ref.shape, -jnp.inf, F32)
    acc_ref[...] = jnp.zeros(acc_ref.shape, F32)

    def stage_a(t):
        kt = k_ref[0, 0, pl.ds(pl.multiple_of(t * tk, tk), tk), :]
        s = jnp.dot(kt, qt_ref[...], preferred_element_type=F32)
        if bias_fn is not None:
            s = s + bias_fn(t)
        s_buf[...] = s
        cm_buf[...] = jnp.max(s, axis=0, keepdims=True)

    def stage_b():
        m_old = m_ref[...]
        m_new = jnp.maximum(m_old, cm_buf[...])
        p_buf[...] = jnp.exp2(s_buf[...] - m_new).astype(BF16)
        a_buf[...] = jnp.exp2(m_old - m_new)
        m_ref[...] = m_new

    def stage_c(t):
        pv = jnp.dot(vt_ref[t][0:PV_ROWS], p_buf[...], preferred_element_type=F32)
        acc_ref[...] = acc_ref[...] * a_buf[...] + pv

    def step(t, do_a=True, do_b=True, do_c=True):
        if do_c:
            stage_c(t - 2)
        if do_b:
            stage_b()
        if do_a:
            stage_a(t)

    step(0, do_b=False, do_c=False)
    step(1, do_c=False)

    def body(t, carry):
        step(t)
        return carry

    lax.fori_loop(2, nk, body, 0)
    step(nk, do_a=False)
    step(nk + 1, do_a=False, do_b=False)


def _mla_attn_kernel(q_ref, k_ref, v_ref, o_ref, vt_ref, qt_ref, *scr, seq, tq, tk, vchunk):
    @pl.when(pl.program_id(2) == 0)
    def _():
        _transpose_v(v_ref, vt_ref, seq=seq, vchunk=vchunk, tk=tk)

    qt_ref[...] = q_ref[0, 0].astype(F32).T.astype(BF16)
    _attn_pipeline(k_ref, vt_ref, qt_ref, None, scr, nk=seq // tk, tk=tk)
    acc = scr[1][...]
    o = acc[0:B_V] / acc[ONES_COL:ONES_COL + 1]
    o_ref[0, 0] = o.T.astype(BF16)


def _diff_attn_kernel(q_ref, k_ref, v_ref, bias_ref, lamp_ref, g_ref, o_ref, vt_ref, qt_ref, *scr,
                      seq, tq, tk, vchunk, n_band, lam_init):
    qi = pl.program_id(2)

    @pl.when(qi == 0)
    def _():
        _transpose_v(v_ref, vt_ref, seq=seq, vchunk=vchunk, tk=tk)

    qt = q_ref[0, 0].astype(F32).T
    row = lax.broadcasted_iota(jnp.int32, (HEAD_PAD, 1), 0)
    qt_ref[:, 0:tq] = jnp.where(row < C_QK, qt, 0.0).astype(BF16)
    qt_ref[:, tq:2 * tq] = jnp.where(row < C_QK, 0.0, qt).astype(BF16)

    j_lo = qi * (tq // tk) - 1

    def bias_fn(t):
        b = bias_ref[0, jnp.clip(t - j_lo + 1, 0, n_band + 1)]
        return jnp.concatenate([b, b], axis=1)

    _attn_pipeline(k_ref, vt_ref, qt_ref, bias_fn, scr, nk=seq // tk, tk=tk)

    lp = lamp_ref[...]
    lam = (jnp.exp(jnp.sum(lp[0:1] * lp[1:2], keepdims=True))
           - jnp.exp(jnp.sum(lp[2:3] * lp[3:4], keepdims=True)) + lam_init)
    acc = scr[1][...]
    o1 = acc[0:C_V, 0:tq] / acc[ONES_COL:ONES_COL + 1, 0:tq]
    o2 = acc[0:C_V, tq:2 * tq] / acc[ONES_COL:ONES_COL + 1, tq:2 * tq]
    o = (o1 - lam * o2).T
    o_ref[0, 0] = (_rms(o, g_ref[...]) * (1.0 - lam_init)).astype(BF16)


def _attn_common(nb, seq, tq, tk, n_cols):
    grid = (nb, B_HEADS, seq // tq)
    q_spec = pl.BlockSpec((1, 1, tq, HEAD_PAD), lambda b, h, i: (b, h, i, 0))
    kv_spec = pl.BlockSpec((1, 1, seq, HEAD_PAD), lambda b, h, i: (b, h, 0, 0))
    out_spec = pl.BlockSpec((1, 1, tq, B_V), lambda b, h, i: (b, h, i, 0))
    out_shape = jax.ShapeDtypeStruct((nb, B_HEADS, seq, B_V), BF16)
    scratch = [pltpu.VMEM((seq // tk, HEAD_PAD, tk), BF16),
               pltpu.VMEM((HEAD_PAD, n_cols), BF16),
               pltpu.VMEM((1, n_cols), F32),
               pltpu.VMEM((PV_ROWS, n_cols), F32),
               pltpu.VMEM((tk, n_cols), F32),
               pltpu.VMEM((1, n_cols), F32),
               pltpu.VMEM((tk, n_cols), BF16),
               pltpu.VMEM((1, n_cols), F32)]
    params = pltpu.CompilerParams(dimension_semantics=("arbitrary",) * 3,
                                  vmem_limit_bytes=VMEM_LIMIT)
    return grid, q_spec, kv_spec, out_spec, out_shape, scratch, params


def _mla_attn(q, k, v, t):
    nb, _, seq, _ = q.shape
    tq, tk = t["tq_mla"], t["tk"]
    grid, q_spec, kv_spec, out_spec, out_shape, scratch, params = _attn_common(nb, seq, tq, tk, tq)
    return pl.pallas_call(
        functools.partial(_mla_attn_kernel, seq=seq, tq=tq, tk=tk, vchunk=t["vchunk"]),
        grid=grid, in_specs=[q_spec, kv_spec, kv_spec], out_specs=out_spec, out_shape=out_shape,
        scratch_shapes=scratch, compiler_params=params, name="mla_attn",
    )(q, k, v)


def _diff_attn(q, k, v, bias_tiles, lam_params, subln_g, lam_init, t):
    nb, _, seq, _ = q.shape
    tq, tk = t["tq_diff"], t["tk"]
    n_band = bias_tiles.shape[1] - 2
    grid, q_spec, kv_spec, out_spec, out_shape, scratch, params = _attn_common(nb, seq, tq, tk, 2 * tq)
    return pl.pallas_call(
        functools.partial(_diff_attn_kernel, seq=seq, tq=tq, tk=tk, vchunk=t["vchunk"],
                          n_band=n_band, lam_init=lam_init),
        grid=grid,
        in_specs=[q_spec, kv_spec, kv_spec,
                  pl.BlockSpec((1, n_band + 2, tk, tq), lambda b, h, i: (h, 0, 0, 0)),
                  pl.BlockSpec((8, LANES), lambda b, h, i: (0, 0)),
                  pl.BlockSpec((1, C_V), lambda b, h, i: (0, 0))],
        out_specs=out_spec, out_shape=out_shape, scratch_shapes=scratch, compiler_params=params,
        name="diff_attn",
    )(q, k, v, bias_tiles, lam_params, subln_g)


def _bias_kernel(bucket_ref, rb_ref, o_ref):
    hd = pl.program_id(0)
    bucket = bucket_ref[0]
    acc = jnp.zeros(bucket.shape, F32)
    for b in range(REL_BUCKETS):
        acc = jnp.where(bucket == b, rb_ref[b, hd], acc)
    o_ref[0, 0] = acc * LOG2E


def _bias_tiles(bucket_t, rel_bias):
    n_band, tk, tq = bucket_t.shape
    return pl.pallas_call(
        _bias_kernel,
        grid=(C_HEADS, n_band),
        in_specs=[pl.BlockSpec((1, tk, tq), lambda h, i: (i, 0, 0)),
                  pl.BlockSpec(memory_space=pltpu.SMEM)],
        out_specs=pl.BlockSpec((1, 1, tk, tq), lambda h, i: (h, i, 0, 0)),
        out_shape=jax.ShapeDtypeStruct((C_HEADS, n_band, tk, tq), F32),
        name="bias_tiles",
    )(bucket_t, rel_bias)


def _post_kernel(x_ref, mixa_ref, ob_ref, oc_ref, wout_ref, mod_ref, n2g_ref, rw_ref, rb_ref,
                 x2_ref, h2_ref, idx_ref, gate_ref, rank_ref, cnt_ref, mix_scr, base_scr, *, tm):
    @pl.when((pl.program_id(0) == 0) & (pl.program_id(1) == 0))
    def _():
        base_scr[...] = jnp.zeros(base_scr.shape, F32)

    mix_scr[:, 0:A_WIDTH] = mixa_ref[0]
    for hd in range(B_HEADS):
        mix_scr[:, A_WIDTH + hd * B_V:A_WIDTH + (hd + 1) * B_V] = ob_ref[0, hd]
    off_c = A_WIDTH + B_HEADS * B_V
    for hd in range(C_HEADS):
        mix_scr[:, off_c + hd * C_V:off_c + (hd + 1) * C_V] = oc_ref[0, hd]
    y = jnp.dot(mix_scr[...], wout_ref[...], preferred_element_type=F32)
    mod = mod_ref[0]
    x2 = x_ref[0] + mod[2:3] * y
    x2_ref[0] = x2
    h2 = _rms(x2, n2g_ref[...]) * (1.0 + mod[4:5]) + mod[3:4]
    h2_ref[0] = h2

    logits = jnp.dot(h2, rw_ref[...], precision=HIGHEST, preferred_element_type=F32) + rb_ref[...]
    lane = lax.broadcasted_iota(jnp.int32, (tm, LANES), 1)
    vals, idxs = [], []
    cur = logits
    for _ in range(TOP_K):
        mx = jnp.max(cur, axis=1, keepdims=True)
        ik = jnp.min(jnp.where(cur == mx, lane, LANES), axis=1, keepdims=True)
        vals.append(mx)
        idxs.append(ik)
        cur = jnp.where(lane == ik, NEG_BIG, cur)
    ex = [jnp.exp(vk - vals[0]) for vk in vals]
    den = ex[0] + ex[1] + ex[2] + ex[3]

    onehots = [(lane == ik).astype(F32) for ik in idxs]
    esum = onehots[0] + onehots[1] + onehots[2] + onehots[3]
    r_i = lax.broadcasted_iota(jnp.int32, (tm, tm), 0)
    c_i = lax.broadcasted_iota(jnp.int32, (tm, tm), 1)
    lower = jnp.where(r_i > c_i, 1.0, 0.0).astype(BF16)
    before = base_scr[...] + jnp.dot(lower, esum.astype(BF16), preferred_element_type=F32)
    idx_out = jnp.zeros((tm, LANES), jnp.int32)
    gate_out = jnp.zeros((tm, LANES), F32)
    rank_out = jnp.zeros((tm, LANES), jnp.int32)
    for k in range(TOP_K):
        rk = jnp.sum(onehots[k] * before, axis=1, keepdims=True).astype(jnp.int32)
        idx_out = jnp.where(lane == k, idxs[k], idx_out)
        gate_out = jnp.where(lane == k, ex[k] / den, gate_out)
        rank_out = jnp.where(lane == k, rk, rank_out)
    idx_ref[0] = idx_out
    gate_ref[0] = gate_out
    rank_ref[0] = rank_out
    base_new = base_scr[...] + jnp.sum(esum, axis=0, keepdims=True)
    base_scr[...] = base_new
    cnt_ref[...] = base_new


def _post(x, mixa, ob, oc, mod_l, lw, t):
    nb, seq, _ = x.shape
    tm = t["tm"]
    full = lambda a: pl.BlockSpec(a.shape, lambda b, i: (0,) * a.ndim)
    tok = lambda w: pl.BlockSpec((1, tm, w), lambda b, i: (b, i, 0))
    head_in = pl.BlockSpec((1, B_HEADS, tm, B_V), lambda b, i: (b, 0, i, 0))
    return pl.pallas_call(
        functools.partial(_post_kernel, tm=tm),
        grid=(nb, seq // tm),
        in_specs=[tok(D_MODEL), tok(A_WIDTH), head_in, head_in, full(lw["wout"]),
                  pl.BlockSpec((1, 6, D_MODEL), lambda b, i: (b, 0, 0)),
                  full(lw["n2g"]), full(lw["rw"]), full(lw["rb"])],
        out_specs=[tok(D_MODEL), tok(D_MODEL), tok(LANES), tok(LANES), tok(LANES),
                   pl.BlockSpec((1, LANES), lambda b, i: (0, 0))],
        out_shape=[jax.ShapeDtypeStruct((nb, seq, D_MODEL), F32),
                   jax.ShapeDtypeStruct((nb, seq, D_MODEL), F32),
                   jax.ShapeDtypeStruct((nb, seq, LANES), jnp.int32),
                   jax.ShapeDtypeStruct((nb, seq, LANES), F32),
                   jax.ShapeDtypeStruct((nb, seq, LANES), jnp.int32),
                   jax.ShapeDtypeStruct((1, LANES), F32)],
        scratch_shapes=[pltpu.VMEM((tm, D_MODEL), BF16), pltpu.VMEM((1, LANES), F32)],
        compiler_params=pltpu.CompilerParams(dimension_semantics=("arbitrary", "arbitrary"),
                                             vmem_limit_bytes=VMEM_LIMIT),
        name="post",
    )(x, mixa, ob, oc, lw["wout"], mod_l, lw["n2g"], lw["rw"], lw["rb"])


def _row_copy(src, src_row, dst, dst_row, sem):
    return pltpu.make_async_copy(src.at[pl.ds(src_row, 1)], dst.at[pl.ds(dst_row, 1)], sem)


def _dispatch_kernel(pos_ref, h2_ref, xs_ref, sem, *, td):
    def start(r, carry):
        for k in range(TOP_K):
            _row_copy(h2_ref, r, xs_ref, pos_ref[0, 0, r * TOP_K + k], sem).start()
        return carry

    def wait(r, carry):
        for k in range(TOP_K):
            _row_copy(h2_ref, 0, xs_ref, 0, sem).wait()
        return carry

    lax.fori_loop(0, td, start, 0)
    lax.fori_loop(0, td, wait, 0)


def _dispatch(pos, h2, n_slots, t):
    n_tok = h2.shape[0]
    td = t["td"]
    return pl.pallas_call(
        functools.partial(_dispatch_kernel, td=td),
        grid=(n_tok // td,),
        in_specs=[pl.BlockSpec((1, 1, td * TOP_K), lambda i: (i, 0, 0), memory_space=pltpu.SMEM),
                  pl.BlockSpec((td, D_MODEL), lambda i: (i, 0))],
        out_specs=pl.BlockSpec(memory_space=pl.ANY),
        out_shape=jax.ShapeDtypeStruct((n_slots, D_MODEL), F32),
        scratch_shapes=[pltpu.SemaphoreType.DMA(())],
        compiler_params=pltpu.CompilerParams(dimension_semantics=("arbitrary",)),
        name="dispatch",
    )(pos.reshape(n_tok // td, 1, td * TOP_K), h2)


def _expert_kernel(blk_e_ref, valid_ref, first_ref, xs_ref, w1_ref, b1_ref, w2_ref, b2_ref, y_ref,
                   w1_scr, w2_scr, *, tmo):
    del blk_e_ref
    i = pl.program_id(0)

    @pl.when(first_ref[i] == 1)
    def _():
        w1_scr[...] = w1_ref[0].astype(BF16)
        w2_scr[...] = w2_ref[0].astype(BF16)

    @pl.when(valid_ref[i] > 0)
    def _():
        row = lax.broadcasted_iota(jnp.int32, (tmo, 1), 0)
        x = jnp.where(row < valid_ref[i], xs_ref[...], 0.0).astype(BF16)
        hmid = jnp.dot(x, w1_scr[...], preferred_element_type=F32) + b1_ref[0]
        gate = jnp.minimum(hmid[:, 0:D_FF], SWIGLU_LIMIT)
        up = jnp.clip(hmid[:, D_FF:2 * D_FF], -SWIGLU_LIMIT, SWIGLU_LIMIT)
        act = (up + 1.0) * (gate * jax.nn.sigmoid(SWIGLU_ALPHA * gate))
        y_ref[...] = jnp.dot(act.astype(BF16), w2_scr[...], preferred_element_type=F32) + b2_ref[0]

    @pl.when(valid_ref[i] == 0)
    def _():
        y_ref[...] = jnp.zeros(y_ref.shape, F32)


def _experts(blk_e, valid, first, xs, w1, b1, w2, b2, layer, t):
    n_slots = xs.shape[0]
    tmo = t["tmo"]
    off = layer * N_EXPERTS
    grid_spec = pltpu.PrefetchScalarGridSpec(
        num_scalar_prefetch=3,
        grid=(n_slots // tmo,),
        in_specs=[pl.BlockSpec((tmo, D_MODEL), lambda i, e, v, f: (i, 0)),
                  pl.BlockSpec((1, D_MODEL, 2 * D_FF), lambda i, e, v, f: (off + e[i], 0, 0)),
                  pl.BlockSpec((1, 1, 2 * D_FF), lambda i, e, v, f: (off + e[i], 0, 0)),
                  pl.BlockSpec((1, D_FF, D_MODEL), lambda i, e, v, f: (off + e[i], 0, 0)),
                  pl.BlockSpec((1, 1, D_MODEL), lambda i, e, v, f: (off + e[i], 0, 0))],
        out_specs=pl.BlockSpec((tmo, D_MODEL), lambda i, e, v, f: (i, 0)),
        scratch_shapes=[pltpu.VMEM((D_MODEL, 2 * D_FF), BF16), pltpu.VMEM((D_FF, D_MODEL), BF16)],
    )
    return pl.pallas_call(
        functools.partial(_expert_kernel, tmo=tmo),
        grid_spec=grid_spec,
        out_shape=jax.ShapeDtypeStruct((n_slots, D_MODEL), F32),
        compiler_params=pltpu.CompilerParams(dimension_semantics=("arbitrary",),
                                             vmem_limit_bytes=VMEM_LIMIT),
        name="experts",
    )(blk_e, valid, first, xs, w1, b1, w2, b2)


def _combine_kernel(pos_ref, gate_ref, x2_ref, mod_ref, fg_ref, y_ref, o_ref, buf, sem, *, tc, final):
    def start(r, carry):
        for k in range(TOP_K):
            _row_copy(y_ref, pos_ref[0, 0, r * TOP_K + k], buf.at[k], r, sem).start()
        return carry

    def wait(r, carry):
        for k in range(TOP_K):
            _row_copy(y_ref, 0, buf.at[k], 0, sem).wait()
        return carry

    lax.fori_loop(0, tc, start, 0)
    lax.fori_loop(0, tc, wait, 0)
    gates = gate_ref[...]
    moe = buf[0] * gates[:, 0:1]
    for k in range(1, TOP_K):
        moe = moe + buf[k] * gates[:, k:k + 1]
    out = x2_ref[...] + mod_ref[0][5:6] * moe
    if final:
        out = _rms(out, fg_ref[...])
    o_ref[...] = out


def _combine(pos, gates, x2, mod_l, final_g, y, seq, t, final, tok_range):
    n_tok = x2.shape[0]
    tc = t["tc"]
    per_b = seq // tc
    first_tile = tok_range[0] // tc
    n_out = tok_range[1] - tok_range[0]
    return pl.pallas_call(
        functools.partial(_combine_kernel, tc=tc, final=final),
        grid=(n_out // tc,),
        in_specs=[pl.BlockSpec((1, 1, tc * TOP_K), lambda i: (i + first_tile, 0, 0),
                               memory_space=pltpu.SMEM),
                  pl.BlockSpec((tc, LANES), lambda i: (i + first_tile, 0)),
                  pl.BlockSpec((tc, D_MODEL), lambda i: (i + first_tile, 0)),
                  pl.BlockSpec((1, 6, D_MODEL), lambda i: ((i + first_tile) // per_b, 0, 0)),
                  pl.BlockSpec((1, D_MODEL), lambda i: (0, 0)),
                  pl.BlockSpec(memory_space=pl.ANY)],
        out_specs=pl.BlockSpec((tc, D_MODEL), lambda i: (i, 0)),
        out_shape=jax.ShapeDtypeStruct((n_out, D_MODEL), F32),
        scratch_shapes=[pltpu.VMEM((TOP_K, tc, D_MODEL), F32), pltpu.SemaphoreType.DMA(())],
        compiler_params=pltpu.CompilerParams(dimension_semantics=("arbitrary",),
                                             vmem_limit_bytes=VMEM_LIMIT),
        name="combine",
    )(pos.reshape(n_tok // tc, 1, tc * TOP_K), gates, x2, mod_l, final_g, y)


def _rot_half_cols(w):
    half = w.shape[-1] // 2
    return jnp.concatenate([-w[..., half:], w[..., :half]], -1)


def _pad_heads(w, n_heads, width):
    lead = w.shape[:-1]
    w = w.reshape(lead + (n_heads, width))
    w = jnp.pad(w, [(0, 0)] * len(lead) + [(0, 0), (0, HEAD_PAD - width)])
    return w.reshape(lead + (n_heads * HEAD_PAD,))


def _rope_block(w):
    return jnp.pad(w, [(0, 0)] * (w.ndim - 1) + [(B_NOPE, HEAD_PAD - B_NOPE - B_ROPE)])


def _pack_layer_weights(w_in, mla_w_uq, mla_w_ukv):
    o = [0, 256, 512, 896, 1152, 1184, 1568, 1952, 2336]
    a = w_in[..., o[0]:o[2]]
    cq = w_in[..., o[2]:o[3]]
    ckv = w_in[..., o[3]:o[4]]
    kr = w_in[..., o[4]:o[5]]
    dq, dk, dv = (w_in[..., o[5]:o[6]], w_in[..., o[6]:o[7]], w_in[..., o[7]:o[8]])
    win = jnp.concatenate([a, cq, ckv, _rope_block(kr), _rope_block(_rot_half_cols(kr)),
                           _pad_heads(dq, C_HEADS, 2 * C_QK), _pad_heads(dk, C_HEADS, 2 * C_QK),
                           _pad_heads(dv, C_HEADS, C_V)], -1).astype(BF16)
    lead = mla_w_uq.shape[:-1]
    uq = mla_w_uq.reshape(lead + (B_HEADS, B_NOPE + B_ROPE))
    nope, rp = uq[..., :B_NOPE], uq[..., B_NOPE:]
    zeros32 = jnp.zeros_like(rp)
    wqa = jnp.concatenate([nope, rp, zeros32], -1).reshape(lead + (B_HEADS * HEAD_PAD,))
    wqb = jnp.concatenate([jnp.zeros_like(nope), _rot_half_cols(rp), zeros32], -1)
    wqb = wqb.reshape(lead + (B_HEADS * HEAD_PAD,))
    lead = mla_w_ukv.shape[:-1]
    ukv = mla_w_ukv.reshape(lead + (B_HEADS, B_NOPE + B_V))
    kn, vv = ukv[..., :B_NOPE], ukv[..., B_NOPE:]
    wkn = jnp.concatenate([kn, jnp.zeros_like(kn)], -1).reshape(lead + (B_HEADS * HEAD_PAD,))
    wv = jnp.concatenate([vv, jnp.zeros_like(vv)], -1).reshape(lead + (B_HEADS * HEAD_PAD,))
    return win, wqa.astype(BF16), wqb.astype(BF16), wkn.astype(BF16), wv.astype(BF16)


def _rope_tables(seq):
    pos = jnp.arange(seq, dtype=F32)
    inv = 1.0 / (ROPE_BASE ** (jnp.arange(B_ROPE // 2, dtype=F32) / (B_ROPE // 2)))
    ang = pos[:, None] * inv[None, :]
    cos = jnp.concatenate([jnp.cos(ang)] * 2, -1)
    sin = jnp.concatenate([jnp.sin(ang)] * 2, -1)
    qscale = (B_NOPE + B_ROPE) ** -0.5 * LOG2E
    ones = jnp.ones((seq, B_NOPE), F32)
    pad = jnp.zeros((seq, HEAD_PAD - B_NOPE - B_ROPE), F32)
    return dict(cosq=jnp.concatenate([ones, cos, pad], -1) * qscale,
                sinq=_rope_block(sin) * qscale,
                cosk=_rope_block(cos), sink=_rope_block(sin))


def _t5_bucket(rel):
    half = REL_BUCKETS // 2
    max_exact = half // 2
    base = jnp.where(rel > 0, half, 0)
    n = jnp.abs(rel)
    nf = jnp.maximum(n, 1).astype(F32)
    large = max_exact + (jnp.log(nf / max_exact) / math.log(REL_MAX_DIST / max_exact)
                         * (half - max_exact)).astype(jnp.int32)
    large = jnp.minimum(large, half - 1)
    return base + jnp.where(n < max_exact, n, large)


def _band_buckets(tq, tk):
    n_band = tq // tk + 2
    kk = jnp.arange(tk)[:, None]
    qq = jnp.arange(tq)[None, :]
    return jnp.stack([_t5_bucket((i - 2) * tk + kk - qq) for i in range(n_band + 2)], 0)


def kernel(x_prompt, x_sample, c_prompt, c_sample, ada_w, ada_b, norm1_g, w_in, gmlp_ln_g, gmlp_ln_b,
           gmlp_ws, gmlp_bs, mla_q_norm_g, mla_w_uq, mla_kv_norm_g, mla_w_ukv, diff_lq1, diff_lk1,
           diff_lq2, diff_lk2, diff_subln_g, w_out, norm2_g, router_w, router_b, moe_w1, moe_b1,
           moe_w2, moe_b2, rel_bias, final_g):
    depth = ada_w.shape[0]
    n_prompt = x_prompt.shape[0]
    assert x_prompt.shape[1] == x_sample.shape[1]
    x = jnp.concatenate([x_prompt, x_sample], 0)
    nb, seq, _ = x.shape
    n_tok = nb * seq
    t = _tiles(seq)
    assert t["tk"] >= REL_MAX_DIST and t["tq_diff"] % t["tk"] == 0 and seq // t["tk"] >= 2

    c = jnp.concatenate([c_prompt, c_sample], 0)
    c_pad = jnp.pad(c, ((0, 8 - nb), (0, 0)))
    mod = _adaln(c_pad, ada_w, ada_b)[:, :nb].reshape(depth, nb, 6, D_MODEL)

    win, wqa, wqb, wkn, wv = _pack_layer_weights(w_in, mla_w_uq, mla_w_ukv)
    ws_stack = gmlp_ws.reshape(depth, A_GROUPS * GMLP_CHUNK, GMLP_CHUNK).astype(BF16)
    bs_tile = jnp.repeat(jnp.swapaxes(gmlp_bs, 1, 2), A_WIDTH // A_GROUPS, axis=2)
    wout16 = w_out.astype(BF16)
    rw_pad = jnp.pad(router_w, ((0, 0), (0, 0), (0, LANES - N_EXPERTS)))
    rb_pad = jnp.pad(router_b, ((0, 0), (0, LANES - N_EXPERTS)), constant_values=NEG_BIG)
    w1_all = moe_w1.reshape(depth * N_EXPERTS, D_MODEL, 2 * D_FF)
    b1_all = moe_b1.reshape(depth * N_EXPERTS, 1, 2 * D_FF)
    w2_all = moe_w2.reshape(depth * N_EXPERTS, D_FF, D_MODEL)
    b2_all = moe_b2.reshape(depth * N_EXPERTS, 1, D_MODEL)
    tabs = _rope_tables(seq)
    bias_tiles = _bias_tiles(_band_buckets(t["tq_diff"], t["tk"]).astype(jnp.int32), rel_bias)
    lam_rows = jnp.stack([diff_lq1, diff_lk1, diff_lq2, diff_lk2], 1)
    lam_rows = jnp.pad(lam_rows, ((0, 0), (0, 4), (0, LANES - C_QK)))

    tmo = t["tmo"]
    n_tiles = -(-(n_tok * TOP_K) // tmo) + N_EXPERTS
    n_slots = n_tiles * tmo
    final_row = final_g.reshape(1, D_MODEL)

    for l in range(depth):
        lw = dict(n1g=norm1_g[l][None], win=win[l], lng=gmlp_ln_g[l][None], lnb=gmlp_ln_b[l][None],
                  ws=ws_stack[l], bs=bs_tile[l], qng=mla_q_norm_g[l][None], wqa=wqa[l], wqb=wqb[l],
                  kvng=mla_kv_norm_g[l][None], wkn=wkn[l], wv=wv[l], wout=wout16[l],
                  n2g=norm2_g[l][None], rw=rw_pad[l], rb=rb_pad[l][None])
        lam_init = 0.8 - 0.6 * math.exp(-0.3 * l)
        mixa, qb, kb, vb, qd, kd, vd = _pre(x, mod[l], lw, tabs, t)
        ob = _mla_attn(qb, kb, vb, t)
        oc = _diff_attn(qd, kd, vd, bias_tiles, lam_rows[l], diff_subln_g[l][None], lam_init, t)
        x2, h2, idx, gates, rank, cnt = _post(x, mixa, ob, oc, mod[l], lw, t)

        counts = cnt[0, :N_EXPERTS].astype(jnp.int32)
        padded = ((counts + tmo - 1) // tmo) * tmo
        pend = jnp.cumsum(padded)
        pstart = pend - padded
        idx4 = idx.reshape(n_tok, LANES)[:, :TOP_K]
        pos = (jnp.take(pstart, idx4) + rank.reshape(n_tok, LANES)[:, :TOP_K]).reshape(-1)
        tile_start = jnp.arange(n_tiles, dtype=jnp.int32) * tmo
        blk_e = jnp.minimum(jnp.sum(tile_start[:, None] >= pend[None, :], axis=1),
                            N_EXPERTS - 1).astype(jnp.int32)
        valid = jnp.clip(jnp.take(pstart + counts, blk_e) - tile_start, 0, tmo).astype(jnp.int32)
        first = jnp.concatenate([jnp.ones((1,), jnp.int32),
                                 (blk_e[1:] != blk_e[:-1]).astype(jnp.int32)])

        xs = _dispatch(pos, h2.reshape(n_tok, D_MODEL), n_slots, t)
        y = _experts(blk_e, valid, first, xs, w1_all, b1_all, w2_all, b2_all, l, t)
        comb = functools.partial(_combine, pos, gates.reshape(n_tok, LANES), x2.reshape(n_tok, D_MODEL),
                                 mod[l], final_row, y, seq, t)
        if l < depth - 1:
            x = comb(False, (0, n_tok)).reshape(nb, seq, D_MODEL)
    n_p = n_prompt * seq
    return (comb(True, (0, n_p)).reshape(n_prompt, seq, D_MODEL),
            comb(True, (n_p, n_tok)).reshape(nb - n_prompt, seq, D_MODEL))
```

```python
import functools
import math

import jax
import jax.numpy as jnp
from jax import lax
from jax.experimental import pallas as pl
from jax.experimental.pallas import tpu as pltpu

F32 = jnp.float32
BF16 = jnp.bfloat16
HIGHEST = lax.Precision.HIGHEST
LOG2E = 1.4426950408889634

D_MODEL = 1024
EPS = 1e-6
GMLP_CHUNK = 128
A_GROUPS = 4
A_WIDTH = 256
B_HEADS = 6
B_Q_LORA = 384
B_KV_LORA = 256
B_NOPE = 64
B_ROPE = 32
B_V = 64
ROPE_BASE = 10000.0
C_HEADS = 6
C_QK = 32
C_V = 64
REL_BUCKETS = 32
REL_MAX_DIST = 128
N_EXPERTS = 32
TOP_K = 4
D_FF = 1024
SWIGLU_LIMIT = 7.0
SWIGLU_ALPHA = 1.702

LANES = 128
HEAD_PAD = 128
ONES_COL = 64
PV_ROWS = 80
NEG_BIG = -3.0e38
VMEM_LIMIT = 56 * 1024 * 1024

SEG_A = (0, 512)
SEG_CQ = (512, 896)
SEG_CKV = (896, 1152)
SEG_KR = (1152, 1280)
SEG_KROT = (1280, 1408)
SEG_DQ = (1408, 2176)
SEG_DK = (2176, 2944)
SEG_DV = (2944, 3712)
IN_PACKED = 3712


def _tiles(seq):
    return dict(
        tm=min(512, seq),
        tq_mla=min(4096, seq),
        tq_diff=min(2048, seq),
        tk=min(256, seq),
        vchunk=min(512, seq),
        td=min(256, seq),
        tmo=256,
        tc=min(256, seq),
    )


def _rms(x, g):
    return x * lax.rsqrt(jnp.mean(x * x, -1, keepdims=True) + EPS) * g


def _adaln_kernel(c_ref, w_ref, b_ref, o_ref):
    c = c_ref[...]
    cs = c * jax.nn.sigmoid(c)
    o_ref[0] = jnp.dot(cs, w_ref[0], precision=HIGHEST, preferred_element_type=F32) + b_ref[0]


def _adaln(c_pad, ada_w, ada_b):
    depth = ada_w.shape[0]
    nb = 6
    return pl.pallas_call(
        _adaln_kernel,
        grid=(depth, nb),
        in_specs=[
            pl.BlockSpec((8, D_MODEL), lambda l, j: (0, 0)),
            pl.BlockSpec((1, D_MODEL, D_MODEL), lambda l, j: (l, 0, j)),
            pl.BlockSpec((1, 1, D_MODEL), lambda l, j: (l, 0, j)),
        ],
        out_specs=pl.BlockSpec((1, 8, D_MODEL), lambda l, j: (l, 0, j)),
        out_shape=jax.ShapeDtypeStruct((depth, 8, 6 * D_MODEL), F32),
        compiler_params=pltpu.CompilerParams(dimension_semantics=("arbitrary", "arbitrary")),
        name="adaln",
    )(c_pad, ada_w, ada_b.reshape(depth, 1, 6 * D_MODEL))


def _pre_kernel(x_ref, mod_ref, n1g_ref, win_ref, lng_ref, lnb_ref, ws_ref, bs_ref,
                qng_ref, wqa_ref, wqb_ref, kvng_ref, wkn_ref, wv_ref,
                cosq_ref, sinq_ref, cosk_ref, sink_ref,
                mixa_ref, qb_ref, kb_ref, vb_ref, qd_ref, kd_ref, vd_ref, *, tm):
    x = x_ref[0]
    mod = mod_ref[0]
    h = (_rms(x, n1g_ref[...]) * (1.0 + mod[1:2]) + mod[0:1]).astype(BF16)

    def seg(s):
        return jnp.dot(h, win_ref[:, s[0]:s[1]], preferred_element_type=F32)

    za = seg(SEG_A)
    u = jax.nn.gelu(za[:, 0:A_WIDTH])
    v = jax.nn.gelu(za[:, A_WIDTH:2 * A_WIDTH])
    vc = v - jnp.mean(v, -1, keepdims=True)
    v = vc * lax.rsqrt(jnp.mean(vc * vc, -1, keepdims=True) + EPS) * lng_ref[...] + lnb_ref[...]
    v16 = v.astype(BF16)
    grp = lax.broadcasted_iota(jnp.int32, (1, A_WIDTH), 1) // (A_WIDTH // A_GROUPS)
    for n in range(tm // GMLP_CHUNK):
        rows = slice(n * GMLP_CHUNK, (n + 1) * GMLP_CHUNK)
        r = jnp.dot(ws_ref[...], v16[rows], preferred_element_type=F32)
        sv = bs_ref[...]
        for g in range(A_GROUPS):
            sv = sv + jnp.where(grp == g, r[g * GMLP_CHUNK:(g + 1) * GMLP_CHUNK], 0.0)
        mixa_ref[0, rows, :] = (u[rows] * sv).astype(BF16)

    lane = lax.broadcasted_iota(jnp.int32, (1, HEAD_PAD), 1)
    ones_col = jnp.where(lane == ONES_COL, 1.0, 0.0).astype(F32)

    cqn = _rms(seg(SEG_CQ), qng_ref[...]).astype(BF16)
    qa = jnp.dot(cqn, wqa_ref[...], preferred_element_type=F32)
    qb = jnp.dot(cqn, wqb_ref[...], preferred_element_type=F32)
    cosq = cosq_ref[...]
    sinq = sinq_ref[...]
    for hd in range(B_HEADS):
        cols = slice(hd * HEAD_PAD, (hd + 1) * HEAD_PAD)
        qb_ref[0, hd] = (qa[:, cols] * cosq + qb[:, cols] * sinq).astype(BF16)
    ckvn = _rms(seg(SEG_CKV), kvng_ref[...]).astype(BF16)
    kn = jnp.dot(ckvn, wkn_ref[...], preferred_element_type=F32)
    vv = jnp.dot(ckvn, wv_ref[...], preferred_element_type=F32)
    krp = seg(SEG_KR) * cosk_ref[...] + seg(SEG_KROT) * sink_ref[...]
    for hd in range(B_HEADS):
        cols = slice(hd * HEAD_PAD, (hd + 1) * HEAD_PAD)
        kb_ref[0, hd] = (kn[:, cols] + krp).astype(BF16)
        vb_ref[0, hd] = (vv[:, cols] + ones_col).astype(BF16)

    zdq = seg(SEG_DQ) * (C_QK ** -0.5 * LOG2E)
    zdk = seg(SEG_DK)
    zdv = seg(SEG_DV)
    for hd in range(C_HEADS):
        cols = slice(hd * HEAD_PAD, (hd + 1) * HEAD_PAD)
        qd_ref[0, hd] = zdq[:, cols].astype(BF16)
        kd_ref[0, hd] = zdk[:, cols].astype(BF16)
        vd_ref[0, hd] = (zdv[:, cols] + ones_col).astype(BF16)


def _pre(x, mod_l, lw, tabs, t):
    nb, seq, _ = x.shape
    tm = t["tm"]
    full = lambda a: pl.BlockSpec(a.shape, lambda b, i: (0,) * a.ndim)
    tab = pl.BlockSpec((tm, HEAD_PAD), lambda b, i: (i, 0))
    head_out = pl.BlockSpec((1, B_HEADS, tm, HEAD_PAD), lambda b, i: (b, 0, i, 0))
    head_shape = jax.ShapeDtypeStruct((nb, B_HEADS, seq, HEAD_PAD), BF16)
    consts = [lw["n1g"], lw["win"], lw["lng"], lw["lnb"], lw["ws"], lw["bs"],
              lw["qng"], lw["wqa"], lw["wqb"], lw["kvng"], lw["wkn"], lw["wv"]]
    return pl.pallas_call(
        functools.partial(_pre_kernel, tm=tm),
        grid=(nb, seq // tm),
        in_specs=[pl.BlockSpec((1, tm, D_MODEL), lambda b, i: (b, i, 0)),
                  pl.BlockSpec((1, 6, D_MODEL), lambda b, i: (b, 0, 0))]
                 + [full(a) for a in consts] + [tab] * 4,
        out_specs=[pl.BlockSpec((1, tm, A_WIDTH), lambda b, i: (b, i, 0))] + [head_out] * 6,
        out_shape=[jax.ShapeDtypeStruct((nb, seq, A_WIDTH), BF16)] + [head_shape] * 6,
        compiler_params=pltpu.CompilerParams(dimension_semantics=("arbitrary", "arbitrary"),
                                             vmem_limit_bytes=VMEM_LIMIT),
        name="pre",
    )(x, mod_l, *consts, tabs["cosq"], tabs["sinq"], tabs["cosk"], tabs["sink"])


def _transpose_v(v_ref, vt_ref, *, seq, vchunk, tk):
    per = vchunk // tk

    def body(i, carry):
        blk = v_ref[0, 0, pl.ds(pl.multiple_of(i * vchunk, vchunk), vchunk), :]
        blk_t = blk.astype(F32).T.astype(BF16)
        for c in range(per):
            vt_ref[i * per + c] = blk_t[:, c * tk:(c + 1) * tk]
        return carry

    lax.fori_loop(0, seq // vchunk, body, 0)


def _attn_pipeline(k_ref, vt_ref, qt_ref, scr, run_steps, *, nk, tk, use_shift):
    m_ref, acc_ref, s_buf, cm_buf, c_buf, p_buf, a_buf = scr
    m_ref[...] = jnp.full(m_ref.shape, NEG_BIG, F32)
    acc_ref[...] = jnp.zeros(acc_ref.shape, F32)
    s_buf[...] = jnp.full(s_buf.shape, -jnp.inf, F32)
    cm_buf[...] = jnp.full(cm_buf.shape, NEG_BIG, F32)
    c_buf[...] = jnp.zeros(c_buf.shape, F32)
    p_buf[...] = jnp.zeros(p_buf.shape, BF16)
    a_buf[...] = jnp.ones(a_buf.shape, F32)

    def stage_a(t, shift, bias):
        kt = k_ref[0, 0, pl.ds(pl.multiple_of(t * tk, tk), tk), :]
        s = jnp.dot(kt, qt_ref[...], preferred_element_type=F32)
        if bias is not None:
            s = s + bias
        s_buf[...] = s
        cm = jnp.max(s, axis=0, keepdims=True)
        if use_shift:
            c = jnp.zeros(c_buf.shape, F32) if shift is None else jnp.full(c_buf.shape, shift, F32)
            c_buf[...] = c
            cm = cm + c
        cm_buf[...] = cm

    def stage_b():
        m_old = m_ref[...]
        m_new = jnp.maximum(m_old, cm_buf[...])
        ref = m_new - c_buf[...] if use_shift else m_new
        p_buf[...] = jnp.exp2(s_buf[...] - ref).astype(BF16)
        a_buf[...] = jnp.exp2(m_old - m_new)
        m_ref[...] = m_new

    def stage_c(t):
        pv = jnp.dot(vt_ref[t][0:PV_ROWS], p_buf[...], preferred_element_type=F32)
        acc_ref[...] = acc_ref[...] * a_buf[...] + pv

    def step(t, shift=None, bias=None, do_a=True, do_b=True):
        stage_c(jnp.maximum(t - 2, 0))
        if do_b:
            stage_b()
        if do_a:
            stage_a(t, shift, bias)

    run_steps(step)
    step(nk, do_a=False)
    step(nk + 1, do_a=False, do_b=False)


def _mla_attn_kernel(q_ref, k_ref, v_ref, o_ref, vt_ref, qt_ref, *scr, seq, tq, tk, vchunk):
    @pl.when(pl.program_id(2) == 0)
    def _():
        _transpose_v(v_ref, vt_ref, seq=seq, vchunk=vchunk, tk=tk)

    qt_ref[...] = q_ref[0, 0].astype(F32).T.astype(BF16)
    nk = seq // tk

    def run_steps(step):
        def body(t, carry):
            step(t)
            return carry
        lax.fori_loop(0, nk, body, 0)

    _attn_pipeline(k_ref, vt_ref, qt_ref, scr, run_steps, nk=nk, tk=tk, use_shift=False)
    acc = scr[1][...]
    o = acc[0:B_V] / acc[ONES_COL:ONES_COL + 1]
    o_ref[0, 0] = o.T.astype(BF16)


def _diff_attn_kernel(q_ref, k_ref, v_ref, bias_ref, cfar_ref, lamp_ref, g_ref, o_ref, vt_ref, qt_ref,
                      *scr, seq, tq, tk, vchunk, lam_init):
    hd = pl.program_id(1)
    qi = pl.program_id(2)

    @pl.when(qi == 0)
    def _():
        _transpose_v(v_ref, vt_ref, seq=seq, vchunk=vchunk, tk=tk)

    qt = q_ref[0, 0].astype(F32).T
    row = lax.broadcasted_iota(jnp.int32, (HEAD_PAD, 1), 0)
    qt_ref[:, 0:tq] = jnp.where(row < C_QK, qt, 0.0).astype(BF16)
    qt_ref[:, tq:2 * tq] = jnp.where(row < C_QK, 0.0, qt).astype(BF16)

    nk = seq // tk
    chunks = tq // tk
    n_band = chunks + 2
    j_lo = qi * chunks - 1
    c_left = cfar_ref[hd, 0]
    c_right = cfar_ref[hd, 1]

    def band_bias(i):
        cols = []
        for c in range(chunks):
            e = i - 1 - c
            if e <= -2:
                cols.append(jnp.full((tk, tk), c_left, F32))
            elif e >= 2:
                cols.append(jnp.full((tk, tk), c_right, F32))
            else:
                cols.append(bias_ref[0, e + 1])
        b = jnp.concatenate(cols, axis=1) if chunks > 1 else cols[0]
        return jnp.concatenate([b, b], axis=1)

    def run_steps(step):
        def far(shift):
            def body(t, carry):
                step(t, shift=shift)
                return carry
            return body

        lax.fori_loop(0, jnp.clip(j_lo, 0, nk), far(c_left), 0)
        for i in range(n_band):
            j = j_lo + i

            @pl.when((j >= 0) & (j < nk))
            def _():
                step(j, bias=band_bias(i))
        lax.fori_loop(jnp.clip(j_lo + n_band, 0, nk), nk, far(c_right), 0)

    _attn_pipeline(k_ref, vt_ref, qt_ref, scr, run_steps, nk=nk, tk=tk, use_shift=True)

    lp = lamp_ref[...]
    lam = (jnp.exp(jnp.sum(lp[0:1] * lp[1:2], keepdims=True))
           - jnp.exp(jnp.sum(lp[2:3] * lp[3:4], keepdims=True)) + lam_init)
    acc = scr[1][...]
    o1 = acc[0:C_V, 0:tq] / acc[ONES_COL:ONES_COL + 1, 0:tq]
    o2 = acc[0:C_V, tq:2 * tq] / acc[ONES_COL:ONES_COL + 1, tq:2 * tq]
    o = (o1 - lam * o2).T
    o_ref[0, 0] = (_rms(o, g_ref[...]) * (1.0 - lam_init)).astype(BF16)


def _attn_common(nb, seq, tq, tk, n_cols):
    grid = (nb, B_HEADS, seq // tq)
    q_spec = pl.BlockSpec((1, 1, tq, HEAD_PAD), lambda b, h, i: (b, h, i, 0))
    kv_spec = pl.BlockSpec((1, 1, seq, HEAD_PAD), lambda b, h, i: (b, h, 0, 0))
    out_spec = pl.BlockSpec((1, 1, tq, B_V), lambda b, h, i: (b, h, i, 0))
    out_shape = jax.ShapeDtypeStruct((nb, B_HEADS, seq, B_V), BF16)
    scratch = [pltpu.VMEM((seq // tk, HEAD_PAD, tk), BF16),
               pltpu.VMEM((HEAD_PAD, n_cols), BF16),
               pltpu.VMEM((1, n_cols), F32),
               pltpu.VMEM((PV_ROWS, n_cols), F32),
               pltpu.VMEM((tk, n_cols), F32),
               pltpu.VMEM((1, n_cols), F32),
               pltpu.VMEM((1, n_cols), F32),
               pltpu.VMEM((tk, n_cols), BF16),
               pltpu.VMEM((1, n_cols), F32)]
    params = pltpu.CompilerParams(dimension_semantics=("arbitrary",) * 3,
                                  vmem_limit_bytes=VMEM_LIMIT)
    return grid, q_spec, kv_spec, out_spec, out_shape, scratch, params


def _mla_attn(q, k, v, t):
    nb, _, seq, _ = q.shape
    tq, tk = t["tq_mla"], t["tk"]
    grid, q_spec, kv_spec, out_spec, out_shape, scratch, params = _attn_common(nb, seq, tq, tk, tq)
    return pl.pallas_call(
        functools.partial(_mla_attn_kernel, seq=seq, tq=tq, tk=tk, vchunk=t["vchunk"]),
        grid=grid, in_specs=[q_spec, kv_spec, kv_spec], out_specs=out_spec, out_shape=out_shape,
        scratch_shapes=scratch, compiler_params=params, name="mla_attn",
    )(q, k, v)


def _diff_attn(q, k, v, bias_tiles, cfar, lam_params, subln_g, lam_init, t):
    nb, _, seq, _ = q.shape
    tq, tk = t["tq_diff"], t["tk"]
    grid, q_spec, kv_spec, out_spec, out_shape, scratch, params = _attn_common(nb, seq, tq, tk, 2 * tq)
    return pl.pallas_call(
        functools.partial(_diff_attn_kernel, seq=seq, tq=tq, tk=tk, vchunk=t["vchunk"],
                          lam_init=lam_init),
        grid=grid,
        in_specs=[q_spec, kv_spec, kv_spec,
                  pl.BlockSpec((1, 3, tk, tk), lambda b, h, i: (h, 0, 0, 0)),
                  pl.BlockSpec(memory_space=pltpu.SMEM),
                  pl.BlockSpec((8, LANES), lambda b, h, i: (0, 0)),
                  pl.BlockSpec((1, C_V), lambda b, h, i: (0, 0))],
        out_specs=out_spec, out_shape=out_shape, scratch_shapes=scratch, compiler_params=params,
        name="diff_attn",
    )(q, k, v, bias_tiles, cfar, lam_params, subln_g)


def _bias_kernel(bucket_ref, rb_ref, o_ref):
    hd = pl.program_id(0)
    bucket = bucket_ref[0]
    acc = jnp.zeros(bucket.shape, F32)
    for b in range(REL_BUCKETS):
        acc = jnp.where(bucket == b, rb_ref[b, hd], acc)
    o_ref[0, 0] = acc * LOG2E


def _bias_tiles(bucket_t, rel_bias):
    n_tiles, tk, tq = bucket_t.shape
    return pl.pallas_call(
        _bias_kernel,
        grid=(C_HEADS, n_tiles),
        in_specs=[pl.BlockSpec((1, tk, tq), lambda h, i: (i, 0, 0)),
                  pl.BlockSpec(memory_space=pltpu.SMEM)],
        out_specs=pl.BlockSpec((1, 1, tk, tq), lambda h, i: (h, i, 0, 0)),
        out_shape=jax.ShapeDtypeStruct((C_HEADS, n_tiles, tk, tq), F32),
        name="bias_tiles",
    )(bucket_t, rel_bias)


def _post_kernel(x_ref, mixa_ref, ob_ref, oc_ref, wout_ref, mod_ref, n2g_ref, rw_ref, rb_ref,
                 x2_ref, h2_ref, idx_ref, gate_ref, rank_ref, cnt_ref, mix_scr, base_scr, *, tm):
    @pl.when((pl.program_id(0) == 0) & (pl.program_id(1) == 0))
    def _():
        base_scr[...] = jnp.zeros(base_scr.shape, F32)

    mix_scr[:, 0:A_WIDTH] = mixa_ref[0]
    for hd in range(B_HEADS):
        mix_scr[:, A_WIDTH + hd * B_V:A_WIDTH + (hd + 1) * B_V] = ob_ref[0, hd]
    off_c = A_WIDTH + B_HEADS * B_V
    for hd in range(C_HEADS):
        mix_scr[:, off_c + hd * C_V:off_c + (hd + 1) * C_V] = oc_ref[0, hd]
    y = jnp.dot(mix_scr[...], wout_ref[...], preferred_element_type=F32)
    mod = mod_ref[0]
    x2 = x_ref[0] + mod[2:3] * y
    x2_ref[0] = x2
    h2 = _rms(x2, n2g_ref[...]) * (1.0 + mod[4:5]) + mod[3:4]
    h2_ref[0] = h2

    logits = jnp.dot(h2, rw_ref[...], precision=HIGHEST, preferred_element_type=F32) + rb_ref[...]
    lane = lax.broadcasted_iota(jnp.int32, (tm, LANES), 1)
    vals, idxs = [], []
    cur = logits
    for _ in range(TOP_K):
        mx = jnp.max(cur, axis=1, keepdims=True)
        ik = jnp.min(jnp.where(cur == mx, lane, LANES), axis=1, keepdims=True)
        vals.append(mx)
        idxs.append(ik)
        cur = jnp.where(lane == ik, NEG_BIG, cur)
    ex = [jnp.exp(vk - vals[0]) for vk in vals]
    den = ex[0] + ex[1] + ex[2] + ex[3]

    onehots = [(lane == ik).astype(F32) for ik in idxs]
    esum = onehots[0] + onehots[1] + onehots[2] + onehots[3]
    r_i = lax.broadcasted_iota(jnp.int32, (tm, tm), 0)
    c_i = lax.broadcasted_iota(jnp.int32, (tm, tm), 1)
    lower = jnp.where(r_i > c_i, 1.0, 0.0).astype(BF16)
    before = base_scr[...] + jnp.dot(lower, esum.astype(BF16), preferred_element_type=F32)
    idx_out = jnp.zeros((tm, LANES), jnp.int32)
    gate_out = jnp.zeros((tm, LANES), F32)
    rank_out = jnp.zeros((tm, LANES), jnp.int32)
    for k in range(TOP_K):
        rk = jnp.sum(onehots[k] * before, axis=1, keepdims=True).astype(jnp.int32)
        idx_out = jnp.where(lane == k, idxs[k], idx_out)
        gate_out = jnp.where(lane == k, ex[k] / den, gate_out)
        rank_out = jnp.where(lane == k, rk, rank_out)
    idx_ref[0] = idx_out
    gate_ref[0] = gate_out
    rank_ref[0] = rank_out
    base_new = base_scr[...] + jnp.sum(esum, axis=0, keepdims=True)
    base_scr[...] = base_new
    cnt_ref[...] = base_new


def _post(x, mixa, ob, oc, mod_l, lw, t):
    nb, seq, _ = x.shape
    tm = t["tm"]
    full = lambda a: pl.BlockSpec(a.shape, lambda b, i: (0,) * a.ndim)
    tok = lambda w: pl.BlockSpec((1, tm, w), lambda b, i: (b, i, 0))
    head_in = pl.BlockSpec((1, B_HEADS, tm, B_V), lambda b, i: (b, 0, i, 0))
    return pl.pallas_call(
        functools.partial(_post_kernel, tm=tm),
        grid=(nb, seq // tm),
        in_specs=[tok(D_MODEL), tok(A_WIDTH), head_in, head_in, full(lw["wout"]),
                  pl.BlockSpec((1, 6, D_MODEL), lambda b, i: (b, 0, 0)),
                  full(lw["n2g"]), full(lw["rw"]), full(lw["rb"])],
        out_specs=[tok(D_MODEL), tok(D_MODEL), tok(LANES), tok(LANES), tok(LANES),
                   pl.BlockSpec((1, LANES), lambda b, i: (0, 0))],
        out_shape=[jax.ShapeDtypeStruct((nb, seq, D_MODEL), F32),
                   jax.ShapeDtypeStruct((nb, seq, D_MODEL), F32),
                   jax.ShapeDtypeStruct((nb, seq, LANES), jnp.int32),
                   jax.ShapeDtypeStruct((nb, seq, LANES), F32),
                   jax.ShapeDtypeStruct((nb, seq, LANES), jnp.int32),
                   jax.ShapeDtypeStruct((1, LANES), F32)],
        scratch_shapes=[pltpu.VMEM((tm, D_MODEL), BF16), pltpu.VMEM((1, LANES), F32)],
        compiler_params=pltpu.CompilerParams(dimension_semantics=("arbitrary", "arbitrary"),
                                             vmem_limit_bytes=VMEM_LIMIT),
        name="post",
    )(x, mixa, ob, oc, lw["wout"], mod_l, lw["n2g"], lw["rw"], lw["rb"])


def _row_copy(src, src_row, dst, dst_row, sem):
    return pltpu.make_async_copy(src.at[pl.ds(src_row, 1)], dst.at[pl.ds(dst_row, 1)], sem)


def _dispatch_kernel(pos_ref, h2_ref, xs_ref, sem, *, td):
    def start(r, carry):
        for k in range(TOP_K):
            _row_copy(h2_ref, r, xs_ref, pos_ref[0, 0, r * TOP_K + k], sem).start(priority=k % 2)
        return carry

    def wait(r, carry):
        for k in range(TOP_K):
            _row_copy(h2_ref, 0, xs_ref, 0, sem).wait()
        return carry

    lax.fori_loop(0, td, start, 0)
    lax.fori_loop(0, td, wait, 0)


def _dispatch(pos, h2, n_slots, t):
    n_tok = h2.shape[0]
    td = t["td"]
    return pl.pallas_call(
        functools.partial(_dispatch_kernel, td=td),
        grid=(n_tok // td,),
        in_specs=[pl.BlockSpec((1, 1, td * TOP_K), lambda i: (i, 0, 0), memory_space=pltpu.SMEM),
                  pl.BlockSpec((td, D_MODEL), lambda i: (i, 0))],
        out_specs=pl.BlockSpec(memory_space=pl.ANY),
        out_shape=jax.ShapeDtypeStruct((n_slots, D_MODEL), F32),
        scratch_shapes=[pltpu.SemaphoreType.DMA(())],
        compiler_params=pltpu.CompilerParams(dimension_semantics=("arbitrary",)),
        name="dispatch",
    )(pos.reshape(n_tok // td, 1, td * TOP_K), h2)


def _expert_kernel(blk_e_ref, valid_ref, first_ref, xs_ref, w1_ref, b1_ref, w2_ref, b2_ref, y_ref,
                   w1_scr, w2_scr, *, tmo):
    del blk_e_ref
    i = pl.program_id(0)

    @pl.when(first_ref[i] == 1)
    def _():
        w1_scr[...] = w1_ref[0].astype(BF16)
        w2_scr[...] = w2_ref[0].astype(BF16)

    @pl.when(valid_ref[i] > 0)
    def _():
        row = lax.broadcasted_iota(jnp.int32, (tmo, 1), 0)
        x = jnp.where(row < valid_ref[i], xs_ref[...], 0.0).astype(BF16)
        hmid = jnp.dot(x, w1_scr[...], preferred_element_type=F32) + b1_ref[0]
        gate = jnp.minimum(hmid[:, 0:D_FF], SWIGLU_LIMIT)
        up = jnp.clip(hmid[:, D_FF:2 * D_FF], -SWIGLU_LIMIT, SWIGLU_LIMIT)
        act = (up + 1.0) * (gate * jax.nn.sigmoid(SWIGLU_ALPHA * gate))
        y_ref[...] = jnp.dot(act.astype(BF16), w2_scr[...], preferred_element_type=F32) + b2_ref[0]

    @pl.when(valid_ref[i] == 0)
    def _():
        y_ref[...] = jnp.zeros(y_ref.shape, F32)


def _experts(blk_e, valid, first, xs, w1, b1, w2, b2, layer, t):
    n_slots = xs.shape[0]
    tmo = t["tmo"]
    off = layer * N_EXPERTS
    grid_spec = pltpu.PrefetchScalarGridSpec(
        num_scalar_prefetch=3,
        grid=(n_slots // tmo,),
        in_specs=[pl.BlockSpec((tmo, D_MODEL), lambda i, e, v, f: (i, 0)),
                  pl.BlockSpec((1, D_MODEL, 2 * D_FF), lambda i, e, v, f: (off + e[i], 0, 0)),
                  pl.BlockSpec((1, 1, 2 * D_FF), lambda i, e, v, f: (off + e[i], 0, 0)),
                  pl.BlockSpec((1, D_FF, D_MODEL), lambda i, e, v, f: (off + e[i], 0, 0)),
                  pl.BlockSpec((1, 1, D_MODEL), lambda i, e, v, f: (off + e[i], 0, 0))],
        out_specs=pl.BlockSpec((tmo, D_MODEL), lambda i, e, v, f: (i, 0)),
        scratch_shapes=[pltpu.VMEM((D_MODEL, 2 * D_FF), BF16), pltpu.VMEM((D_FF, D_MODEL), BF16)],
    )
    return pl.pallas_call(
        functools.partial(_expert_kernel, tmo=tmo),
        grid_spec=grid_spec,
        out_shape=jax.ShapeDtypeStruct((n_slots, D_MODEL), F32),
        compiler_params=pltpu.CompilerParams(dimension_semantics=("arbitrary",),
                                             vmem_limit_bytes=VMEM_LIMIT),
        name="experts",
    )(blk_e, valid, first, xs, w1, b1, w2, b2)


def _combine_kernel(pos_ref, gate_ref, x2_ref, mod_ref, fg_ref, y_ref, o_ref, buf, sem, *, tc, final):
    def start(r, carry):
        for k in range(TOP_K):
            _row_copy(y_ref, pos_ref[0, 0, r * TOP_K + k], buf.at[k], r, sem).start(priority=k % 2)
        return carry

    def wait(r, carry):
        for k in range(TOP_K):
            _row_copy(y_ref, 0, buf.at[k], 0, sem).wait()
        return carry

    lax.fori_loop(0, tc, start, 0)
    lax.fori_loop(0, tc, wait, 0)
    gates = gate_ref[...]
    moe = buf[0] * gates[:, 0:1]
    for k in range(1, TOP_K):
        moe = moe + buf[k] * gates[:, k:k + 1]
    out = x2_ref[...] + mod_ref[0][5:6] * moe
    if final:
        out = _rms(out, fg_ref[...])
    o_ref[...] = out


def _combine(pos, gates, x2, mod_l, final_g, y, seq, t, final, tok_range):
    n_tok = x2.shape[0]
    tc = t["tc"]
    per_b = seq // tc
    first_tile = tok_range[0] // tc
    n_out = tok_range[1] - tok_range[0]
    return pl.pallas_call(
        functools.partial(_combine_kernel, tc=tc, final=final),
        grid=(n_out // tc,),
        in_specs=[pl.BlockSpec((1, 1, tc * TOP_K), lambda i: (i + first_tile, 0, 0),
                               memory_space=pltpu.SMEM),
                  pl.BlockSpec((tc, LANES), lambda i: (i + first_tile, 0)),
                  pl.BlockSpec((tc, D_MODEL), lambda i: (i + first_tile, 0)),
                  pl.BlockSpec((1, 6, D_MODEL), lambda i: ((i + first_tile) // per_b, 0, 0)),
                  pl.BlockSpec((1, D_MODEL), lambda i: (0, 0)),
                  pl.BlockSpec(memory_space=pl.ANY)],
        out_specs=pl.BlockSpec((tc, D_MODEL), lambda i: (i, 0)),
        out_shape=jax.ShapeDtypeStruct((n_out, D_MODEL), F32),
        scratch_shapes=[pltpu.VMEM((TOP_K, tc, D_MODEL), F32), pltpu.SemaphoreType.DMA(())],
        compiler_params=pltpu.CompilerParams(dimension_semantics=("arbitrary",),
                                             vmem_limit_bytes=VMEM_LIMIT),
        name="combine",
    )(pos.reshape(n_tok // tc, 1, tc * TOP_K), gates, x2, mod_l, final_g, y)


def _rot_half_cols(w):
    half = w.shape[-1] // 2
    return jnp.concatenate([-w[..., half:], w[..., :half]], -1)


def _pad_heads(w, n_heads, width):
    lead = w.shape[:-1]
    w = w.reshape(lead + (n_heads, width))
    w = jnp.pad(w, [(0, 0)] * len(lead) + [(0, 0), (0, HEAD_PAD - width)])
    return w.reshape(lead + (n_heads * HEAD_PAD,))


def _rope_block(w):
    return jnp.pad(w, [(0, 0)] * (w.ndim - 1) + [(B_NOPE, HEAD_PAD - B_NOPE - B_ROPE)])


def _pack_layer_weights(w_in, mla_w_uq, mla_w_ukv):
    o = [0, 256, 512, 896, 1152, 1184, 1568, 1952, 2336]
    a = w_in[..., o[0]:o[2]]
    cq = w_in[..., o[2]:o[3]]
    ckv = w_in[..., o[3]:o[4]]
    kr = w_in[..., o[4]:o[5]]
    dq, dk, dv = (w_in[..., o[5]:o[6]], w_in[..., o[6]:o[7]], w_in[..., o[7]:o[8]])
    win = jnp.concatenate([a, cq, ckv, _rope_block(kr), _rope_block(_rot_half_cols(kr)),
                           _pad_heads(dq, C_HEADS, 2 * C_QK), _pad_heads(dk, C_HEADS, 2 * C_QK),
                           _pad_heads(dv, C_HEADS, C_V)], -1).astype(BF16)
    lead = mla_w_uq.shape[:-1]
    uq = mla_w_uq.reshape(lead + (B_HEADS, B_NOPE + B_ROPE))
    nope, rp = uq[..., :B_NOPE], uq[..., B_NOPE:]
    zeros32 = jnp.zeros_like(rp)
    wqa = jnp.concatenate([nope, rp, zeros32], -1).reshape(lead + (B_HEADS * HEAD_PAD,))
    wqb = jnp.concatenate([jnp.zeros_like(nope), _rot_half_cols(rp), zeros32], -1)
    wqb = wqb.reshape(lead + (B_HEADS * HEAD_PAD,))
    lead = mla_w_ukv.shape[:-1]
    ukv = mla_w_ukv.reshape(lead + (B_HEADS, B_NOPE + B_V))
    kn, vv = ukv[..., :B_NOPE], ukv[..., B_NOPE:]
    wkn = jnp.concatenate([kn, jnp.zeros_like(kn)], -1).reshape(lead + (B_HEADS * HEAD_PAD,))
    wv = jnp.concatenate([vv, jnp.zeros_like(vv)], -1).reshape(lead + (B_HEADS * HEAD_PAD,))
    return win, wqa.astype(BF16), wqb.astype(BF16), wkn.astype(BF16), wv.astype(BF16)


def _rope_tables(seq):
    pos = jnp.arange(seq, dtype=F32)
    inv = 1.0 / (ROPE_BASE ** (jnp.arange(B_ROPE // 2, dtype=F32) / (B_ROPE // 2)))
    ang = pos[:, None] * inv[None, :]
    cos = jnp.concatenate([jnp.cos(ang)] * 2, -1)
    sin = jnp.concatenate([jnp.sin(ang)] * 2, -1)
    qscale = (B_NOPE + B_ROPE) ** -0.5 * LOG2E
    ones = jnp.ones((seq, B_NOPE), F32)
    pad = jnp.zeros((seq, HEAD_PAD - B_NOPE - B_ROPE), F32)
    return dict(cosq=jnp.concatenate([ones, cos, pad], -1) * qscale,
                sinq=_rope_block(sin) * qscale,
                cosk=_rope_block(cos), sink=_rope_block(sin))


def _t5_bucket(rel):
    half = REL_BUCKETS // 2
    max_exact = half // 2
    base = jnp.where(rel > 0, half, 0)
    n = jnp.abs(rel)
    nf = jnp.maximum(n, 1).astype(F32)
    large = max_exact + (jnp.log(nf / max_exact) / math.log(REL_MAX_DIST / max_exact)
                         * (half - max_exact)).astype(jnp.int32)
    large = jnp.minimum(large, half - 1)
    return base + jnp.where(n < max_exact, n, large)


def _band_buckets(tk):
    kk = jnp.arange(tk)[:, None]
    qq = jnp.arange(tk)[None, :]
    return jnp.stack([_t5_bucket(e * tk + kk - qq) for e in (-1, 0, 1)], 0)


def kernel(x_prompt, x_sample, c_prompt, c_sample, ada_w, ada_b, norm1_g, w_in, gmlp_ln_g, gmlp_ln_b,
           gmlp_ws, gmlp_bs, mla_q_norm_g, mla_w_uq, mla_kv_norm_g, mla_w_ukv, diff_lq1, diff_lk1,
           diff_lq2, diff_lk2, diff_subln_g, w_out, norm2_g, router_w, router_b, moe_w1, moe_b1,
           moe_w2, moe_b2, rel_bias, final_g):
    depth = ada_w.shape[0]
    n_prompt = x_prompt.shape[0]
    assert x_prompt.shape[1] == x_sample.shape[1]
    x = jnp.concatenate([x_prompt, x_sample], 0)
    nb, seq, _ = x.shape
    n_tok = nb * seq
    t = _tiles(seq)
    assert t["tk"] >= REL_MAX_DIST and t["tq_diff"] % t["tk"] == 0 and seq // t["tk"] >= 2

    c = jnp.concatenate([c_prompt, c_sample], 0)
    c_pad = jnp.pad(c, ((0, 8 - nb), (0, 0)))
    mod = _adaln(c_pad, ada_w, ada_b)[:, :nb].reshape(depth, nb, 6, D_MODEL)

    win, wqa, wqb, wkn, wv = _pack_layer_weights(w_in, mla_w_uq, mla_w_ukv)
    ws_stack = gmlp_ws.reshape(depth, A_GROUPS * GMLP_CHUNK, GMLP_CHUNK).astype(BF16)
    bs_tile = jnp.repeat(jnp.swapaxes(gmlp_bs, 1, 2), A_WIDTH // A_GROUPS, axis=2)
    wout16 = w_out.astype(BF16)
    rw_pad = jnp.pad(router_w, ((0, 0), (0, 0), (0, LANES - N_EXPERTS)))
    rb_pad = jnp.pad(router_b, ((0, 0), (0, LANES - N_EXPERTS)), constant_values=NEG_BIG)
    w1_all = moe_w1.reshape(depth * N_EXPERTS, D_MODEL, 2 * D_FF)
    b1_all = moe_b1.reshape(depth * N_EXPERTS, 1, 2 * D_FF)
    w2_all = moe_w2.reshape(depth * N_EXPERTS, D_FF, D_MODEL)
    b2_all = moe_b2.reshape(depth * N_EXPERTS, 1, D_MODEL)
    tabs = _rope_tables(seq)
    bias_tiles = _bias_tiles(_band_buckets(t["tk"]).astype(jnp.int32), rel_bias)
    half = REL_BUCKETS // 2
    cfar = jnp.stack([rel_bias[half - 1], rel_bias[REL_BUCKETS - 1]], -1) * LOG2E
    lam_rows = jnp.stack([diff_lq1, diff_lk1, diff_lq2, diff_lk2], 1)
    lam_rows = jnp.pad(lam_rows, ((0, 0), (0, 4), (0, LANES - C_QK)))

    tmo = t["tmo"]
    n_tiles = -(-(n_tok * TOP_K) // tmo) + N_EXPERTS
    n_slots = n_tiles * tmo
    final_row = final_g.reshape(1, D_MODEL)

    for l in range(depth):
        lw = dict(n1g=norm1_g[l][None], win=win[l], lng=gmlp_ln_g[l][None], lnb=gmlp_ln_b[l][None],
                  ws=ws_stack[l], bs=bs_tile[l], qng=mla_q_norm_g[l][None], wqa=wqa[l], wqb=wqb[l],
                  kvng=mla_kv_norm_g[l][None], wkn=wkn[l], wv=wv[l], wout=wout16[l],
                  n2g=norm2_g[l][None], rw=rw_pad[l], rb=rb_pad[l][None])
        lam_init = 0.8 - 0.6 * math.exp(-0.3 * l)
        mixa, qb, kb, vb, qd, kd, vd = _pre(x, mod[l], lw, tabs, t)
        ob = _mla_attn(qb, kb, vb, t)
        oc = _diff_attn(qd, kd, vd, bias_tiles, cfar, lam_rows[l], diff_subln_g[l][None], lam_init, t)
        x2, h2, idx, gates, rank, cnt = _post(x, mixa, ob, oc, mod[l], lw, t)

        counts = cnt[0, :N_EXPERTS].astype(jnp.int32)
        padded = ((counts + tmo - 1) // tmo) * tmo
        pend = jnp.cumsum(padded)
        pstart = pend - padded
        idx4 = idx.reshape(n_tok, LANES)[:, :TOP_K]
        pos = (jnp.take(pstart, idx4) + rank.reshape(n_tok, LANES)[:, :TOP_K]).reshape(-1)
        tile_start = jnp.arange(n_tiles, dtype=jnp.int32) * tmo
        blk_e = jnp.minimum(jnp.sum(tile_start[:, None] >= pend[None, :], axis=1),
                            N_EXPERTS - 1).astype(jnp.int32)
        valid = jnp.clip(jnp.take(pstart + counts, blk_e) - tile_start, 0, tmo).astype(jnp.int32)
        first = jnp.concatenate([jnp.ones((1,), jnp.int32),
                                 (blk_e[1:] != blk_e[:-1]).astype(jnp.int32)])

        xs = _dispatch(pos, h2.reshape(n_tok, D_MODEL), n_slots, t)
        y = _experts(blk_e, valid, first, xs, w1_all, b1_all, w2_all, b2_all, l, t)
        comb = functools.partial(_combine, pos, gates.reshape(n_tok, LANES), x2.reshape(n_tok, D_MODEL),
                                 mod[l], final_row, y, seq, t)
        if l < depth - 1:
            x = comb(False, (0, n_tok)).reshape(nb, seq, D_MODEL)
    n_p = n_prompt * seq
    return (comb(True, (0, n_p)).reshape(n_prompt, seq, D_MODEL),
            comb(True, (n_p, n_tok)).reshape(nb - n_prompt, seq, D_MODEL))
```

```python
import functools
import math

import jax
import jax.numpy as jnp
from jax import lax
from jax.experimental import pallas as pl
from jax.experimental.pallas import tpu as pltpu

F32 = jnp.float32
BF16 = jnp.bfloat16
HIGHEST = lax.Precision.HIGHEST
LOG2E = 1.4426950408889634

D_MODEL = 1024
EPS = 1e-6
GMLP_CHUNK = 128
A_GROUPS = 4
A_WIDTH = 256
B_HEADS = 6
B_Q_LORA = 384
B_KV_LORA = 256
B_NOPE = 64
B_ROPE = 32
B_V = 64
ROPE_BASE = 10000.0
C_HEADS = 6
C_QK = 32
C_V = 64
REL_BUCKETS = 32
REL_MAX_DIST = 128
N_EXPERTS = 32
TOP_K = 4
D_FF = 1024
SWIGLU_LIMIT = 7.0
SWIGLU_ALPHA = 1.702

LANES = 128
HEAD_PAD = 128
ONES_COL = 64
PV_ROWS = 80
NEG_BIG = -3.0e38
VMEM_LIMIT = 56 * 1024 * 1024

SEG_A = (0, 512)
SEG_CQ = (512, 896)
SEG_CKV = (896, 1152)
SEG_KR = (1152, 1280)
SEG_KROT = (1280, 1408)
SEG_DQ = (1408, 2176)
SEG_DK = (2176, 2944)
SEG_DV = (2944, 3712)
IN_PACKED = 3712


def _tiles(seq):
    return dict(
        tm=min(512, seq),
        tq_mla=min(8192, seq),
        tq_diff=min(4096, seq),
        tk=min(256, seq),
        vchunk=min(512, seq),
        td=min(256, seq),
        tmo=256,
        tc=min(256, seq),
    )


def _rms(x, g):
    return x * lax.rsqrt(jnp.mean(x * x, -1, keepdims=True) + EPS) * g


def _adaln_kernel(c_ref, w_ref, b_ref, o_ref):
    c = c_ref[...]
    cs = c * jax.nn.sigmoid(c)
    o_ref[0] = jnp.dot(cs, w_ref[0], precision=HIGHEST, preferred_element_type=F32) + b_ref[0]


def _adaln(c_pad, ada_w, ada_b):
    depth = ada_w.shape[0]
    nb = 6
    return pl.pallas_call(
        _adaln_kernel,
        grid=(depth, nb),
        in_specs=[
            pl.BlockSpec((8, D_MODEL), lambda l, j: (0, 0)),
            pl.BlockSpec((1, D_MODEL, D_MODEL), lambda l, j: (l, 0, j)),
            pl.BlockSpec((1, 1, D_MODEL), lambda l, j: (l, 0, j)),
        ],
        out_specs=pl.BlockSpec((1, 8, D_MODEL), lambda l, j: (l, 0, j)),
        out_shape=jax.ShapeDtypeStruct((depth, 8, 6 * D_MODEL), F32),
        compiler_params=pltpu.CompilerParams(dimension_semantics=("arbitrary", "arbitrary")),
        name="adaln",
    )(c_pad, ada_w, ada_b.reshape(depth, 1, 6 * D_MODEL))


def _pre_kernel(x_ref, mod_ref, n1g_ref, win_ref, lng_ref, lnb_ref, ws_ref, bs_ref,
                qng_ref, wqa_ref, wqb_ref, kvng_ref, wkn_ref, wv_ref,
                cosq_ref, sinq_ref, cosk_ref, sink_ref,
                mixa_ref, qb_ref, kb_ref, vb_ref, qd_ref, kd_ref, vd_ref, *, tm):
    x = x_ref[0]
    mod = mod_ref[0]
    h = (_rms(x, n1g_ref[...]) * (1.0 + mod[1:2]) + mod[0:1]).astype(BF16)

    def seg(s):
        return jnp.dot(h, win_ref[:, s[0]:s[1]], preferred_element_type=F32)

    za = seg(SEG_A)
    u = jax.nn.gelu(za[:, 0:A_WIDTH])
    v = jax.nn.gelu(za[:, A_WIDTH:2 * A_WIDTH])
    vc = v - jnp.mean(v, -1, keepdims=True)
    v = vc * lax.rsqrt(jnp.mean(vc * vc, -1, keepdims=True) + EPS) * lng_ref[...] + lnb_ref[...]
    v16 = v.astype(BF16)
    grp = lax.broadcasted_iota(jnp.int32, (1, A_WIDTH), 1) // (A_WIDTH // A_GROUPS)
    for n in range(tm // GMLP_CHUNK):
        rows = slice(n * GMLP_CHUNK, (n + 1) * GMLP_CHUNK)
        r = jnp.dot(ws_ref[...], v16[rows], preferred_element_type=F32)
        sv = bs_ref[...]
        for g in range(A_GROUPS):
            sv = sv + jnp.where(grp == g, r[g * GMLP_CHUNK:(g + 1) * GMLP_CHUNK], 0.0)
        mixa_ref[0, rows, :] = (u[rows] * sv).astype(BF16)

    lane = lax.broadcasted_iota(jnp.int32, (1, HEAD_PAD), 1)
    ones_col = jnp.where(lane == ONES_COL, 1.0, 0.0).astype(F32)

    cqn = _rms(seg(SEG_CQ), qng_ref[...]).astype(BF16)
    qa = jnp.dot(cqn, wqa_ref[...], preferred_element_type=F32)
    qb = jnp.dot(cqn, wqb_ref[...], preferred_element_type=F32)
    cosq = cosq_ref[...]
    sinq = sinq_ref[...]
    for hd in range(B_HEADS):
        cols = slice(hd * HEAD_PAD, (hd + 1) * HEAD_PAD)
        qb_ref[0, hd] = (qa[:, cols] * cosq + qb[:, cols] * sinq).astype(BF16)
    ckvn = _rms(seg(SEG_CKV), kvng_ref[...]).astype(BF16)
    kn = jnp.dot(ckvn, wkn_ref[...], preferred_element_type=F32)
    vv = jnp.dot(ckvn, wv_ref[...], preferred_element_type=F32)
    krp = seg(SEG_KR) * cosk_ref[...] + seg(SEG_KROT) * sink_ref[...]
    for hd in range(B_HEADS):
        cols = slice(hd * HEAD_PAD, (hd + 1) * HEAD_PAD)
        kb_ref[0, hd] = (kn[:, cols] + krp).astype(BF16)
        vb_ref[0, hd] = (vv[:, cols] + ones_col).astype(BF16)

    zdq = seg(SEG_DQ) * (C_QK ** -0.5 * LOG2E)
    zdk = seg(SEG_DK)
    zdv = seg(SEG_DV)
    for hd in range(C_HEADS):
        cols = slice(hd * HEAD_PAD, (hd + 1) * HEAD_PAD)
        qd_ref[0, hd] = zdq[:, cols].astype(BF16)
        kd_ref[0, hd] = zdk[:, cols].astype(BF16)
        vd_ref[0, hd] = (zdv[:, cols] + ones_col).astype(BF16)


def _pre(x, mod_l, lw, tabs, t):
    nb, seq, _ = x.shape
    tm = t["tm"]
    full = lambda a: pl.BlockSpec(a.shape, lambda b, i: (0,) * a.ndim)
    tab = pl.BlockSpec((tm, HEAD_PAD), lambda b, i: (i, 0))
    head_out = pl.BlockSpec((1, B_HEADS, tm, HEAD_PAD), lambda b, i: (b, 0, i, 0))
    head_shape = jax.ShapeDtypeStruct((nb, B_HEADS, seq, HEAD_PAD), BF16)
    consts = [lw["n1g"], lw["win"], lw["lng"], lw["lnb"], lw["ws"], lw["bs"],
              lw["qng"], lw["wqa"], lw["wqb"], lw["kvng"], lw["wkn"], lw["wv"]]
    return pl.pallas_call(
        functools.partial(_pre_kernel, tm=tm),
        grid=(nb, seq // tm),
        in_specs=[pl.BlockSpec((1, tm, D_MODEL), lambda b, i: (b, i, 0)),
                  pl.BlockSpec((1, 6, D_MODEL), lambda b, i: (b, 0, 0))]
                 + [full(a) for a in consts] + [tab] * 4,
        out_specs=[pl.BlockSpec((1, tm, A_WIDTH), lambda b, i: (b, i, 0))] + [head_out] * 6,
        out_shape=[jax.ShapeDtypeStruct((nb, seq, A_WIDTH), BF16)] + [head_shape] * 6,
        compiler_params=pltpu.CompilerParams(dimension_semantics=("arbitrary", "arbitrary"),
                                             vmem_limit_bytes=VMEM_LIMIT),
        name="pre",
    )(x, mod_l, *consts, tabs["cosq"], tabs["sinq"], tabs["cosk"], tabs["sink"])


def _transpose_v(v_ref, vt_ref, *, seq, vchunk, tk):
    per = vchunk // tk

    def body(i, carry):
        blk = v_ref[0, 0, pl.ds(pl.multiple_of(i * vchunk, vchunk), vchunk), :]
        blk_t = blk.astype(F32).T.astype(BF16)
        for c in range(per):
            vt_ref[i * per + c] = blk_t[:, c * tk:(c + 1) * tk]
        return carry

    lax.fori_loop(0, seq // vchunk, body, 0)


def _attn_pipeline(k_ref, vt_ref, qt_ref, scr, run_steps, *, nk, tk, use_shift):
    m_ref, acc_ref, s_buf, cm_buf, c_buf, p_buf, a_buf = scr
    m_ref[...] = jnp.full(m_ref.shape, NEG_BIG, F32)
    acc_ref[...] = jnp.zeros(acc_ref.shape, F32)
    s_buf[...] = jnp.full(s_buf.shape, -jnp.inf, F32)
    cm_buf[...] = jnp.full(cm_buf.shape, NEG_BIG, F32)
    c_buf[...] = jnp.zeros(c_buf.shape, F32)
    p_buf[...] = jnp.zeros(p_buf.shape, BF16)
    a_buf[...] = jnp.ones(a_buf.shape, F32)

    def stage_a(t, shift, bias):
        kt = k_ref[0, 0, pl.ds(pl.multiple_of(t * tk, tk), tk), :]
        s = jnp.dot(kt, qt_ref[...], preferred_element_type=F32)
        if bias:
            cols = [s[:, c * tk:(c + 1) * tk] for c in range(s.shape[1] // tk)]
            for c, tile in bias:
                cols[c] = cols[c] + tile
            s = jnp.concatenate(cols, axis=1)
        s_buf[...] = s
        cm = jnp.max(s, axis=0, keepdims=True)
        if use_shift:
            c = jnp.broadcast_to(jnp.asarray(shift, F32), c_buf.shape)
            c_buf[...] = c
            cm = cm + c
        cm_buf[...] = cm

    def stage_b():
        m_old = m_ref[...]
        m_new = jnp.maximum(m_old, cm_buf[...])
        ref = m_new - c_buf[...] if use_shift else m_new
        p_buf[...] = jnp.exp2(s_buf[...] - ref).astype(BF16)
        a_buf[...] = jnp.exp2(m_old - m_new)
        m_ref[...] = m_new

    def stage_c(t):
        pv = jnp.dot(vt_ref[t][0:PV_ROWS], p_buf[...], preferred_element_type=F32)
        acc_ref[...] = acc_ref[...] * a_buf[...] + pv

    def step(t, shift=0.0, bias=None, do_a=True, do_b=True):
        stage_c(jnp.maximum(t - 2, 0))
        if do_b:
            stage_b()
        if do_a:
            stage_a(t, shift, bias)

    run_steps(step)
    step(nk, do_a=False)
    step(nk + 1, do_a=False, do_b=False)


def _mla_attn_kernel(q_ref, k_ref, v_ref, o_ref, vt_ref, qt_ref, *scr, seq, tq, tk, vchunk):
    @pl.when(pl.program_id(2) == 0)
    def _():
        _transpose_v(v_ref, vt_ref, seq=seq, vchunk=vchunk, tk=tk)

    qt_ref[...] = q_ref[0, 0].astype(F32).T.astype(BF16)
    nk = seq // tk

    def run_steps(step):
        def body(t, carry):
            step(t)
            return carry
        lax.fori_loop(0, nk, body, 0)

    _attn_pipeline(k_ref, vt_ref, qt_ref, scr, run_steps, nk=nk, tk=tk, use_shift=False)
    acc = scr[1][...]
    o = acc[0:B_V] / acc[ONES_COL:ONES_COL + 1]
    o_ref[0, 0] = o.T.astype(BF16)


def _diff_attn_kernel(q_ref, k_ref, v_ref, bias_ref, cfar_ref, lamp_ref, g_ref, o_ref, vt_ref, qt_ref,
                      *scr, seq, tq, tk, vchunk, lam_init):
    hd = pl.program_id(1)
    qi = pl.program_id(2)

    @pl.when(qi == 0)
    def _():
        _transpose_v(v_ref, vt_ref, seq=seq, vchunk=vchunk, tk=tk)

    qt = q_ref[0, 0].astype(F32).T
    row = lax.broadcasted_iota(jnp.int32, (HEAD_PAD, 1), 0)
    qt_ref[:, 0:tq] = jnp.where(row < C_QK, qt, 0.0).astype(BF16)
    qt_ref[:, tq:2 * tq] = jnp.where(row < C_QK, 0.0, qt).astype(BF16)

    nk = seq // tk
    chunks = tq // tk
    n_band = chunks + 2
    j_lo = qi * chunks - 1
    c_left = cfar_ref[hd, 0]
    c_right = cfar_ref[hd, 1]

    def band_bias(i):
        shifts, parts = [], []
        for c in range(chunks):
            e = i - 1 - c
            if e <= -2:
                shifts.append(jnp.full((1, tk), c_left, F32))
            elif e >= 2:
                shifts.append(jnp.full((1, tk), c_right, F32))
            else:
                shifts.append(jnp.zeros((1, tk), F32))
                parts += [(c, bias_ref[0, e + 1]), (chunks + c, bias_ref[0, e + 1])]
        return jnp.concatenate(shifts + shifts, axis=1), parts

    def run_steps(step):
        def far(shift):
            def body(t, carry):
                step(t, shift=shift)
                return carry
            return body

        lax.fori_loop(0, jnp.clip(j_lo, 0, nk), far(c_left), 0)
        for i in range(n_band):
            j = j_lo + i

            @pl.when((j >= 0) & (j < nk))
            def _():
                shift, parts = band_bias(i)
                step(j, shift=shift, bias=parts)
        lax.fori_loop(jnp.clip(j_lo + n_band, 0, nk), nk, far(c_right), 0)

    _attn_pipeline(k_ref, vt_ref, qt_ref, scr, run_steps, nk=nk, tk=tk, use_shift=True)

    lp = lamp_ref[...]
    lam = (jnp.exp(jnp.sum(lp[0:1] * lp[1:2], keepdims=True))
           - jnp.exp(jnp.sum(lp[2:3] * lp[3:4], keepdims=True)) + lam_init)
    acc = scr[1][...]
    o1 = acc[0:C_V, 0:tq] / acc[ONES_COL:ONES_COL + 1, 0:tq]
    o2 = acc[0:C_V, tq:2 * tq] / acc[ONES_COL:ONES_COL + 1, tq:2 * tq]
    o = (o1 - lam * o2).T
    o_ref[0, 0] = (_rms(o, g_ref[...]) * (1.0 - lam_init)).astype(BF16)


def _attn_common(nb, seq, tq, tk, n_cols):
    grid = (nb, B_HEADS, seq // tq)
    q_spec = pl.BlockSpec((1, 1, tq, HEAD_PAD), lambda b, h, i: (b, h, i, 0))
    kv_spec = pl.BlockSpec((1, 1, seq, HEAD_PAD), lambda b, h, i: (b, h, 0, 0))
    out_spec = pl.BlockSpec((1, 1, tq, B_V), lambda b, h, i: (b, h, i, 0))
    out_shape = jax.ShapeDtypeStruct((nb, B_HEADS, seq, B_V), BF16)
    scratch = [pltpu.VMEM((seq // tk, HEAD_PAD, tk), BF16),
               pltpu.VMEM((HEAD_PAD, n_cols), BF16),
               pltpu.VMEM((1, n_cols), F32),
               pltpu.VMEM((PV_ROWS, n_cols), F32),
               pltpu.VMEM((tk, n_cols), F32),
               pltpu.VMEM((1, n_cols), F32),
               pltpu.VMEM((1, n_cols), F32),
               pltpu.VMEM((tk, n_cols), BF16),
               pltpu.VMEM((1, n_cols), F32)]
    params = pltpu.CompilerParams(dimension_semantics=("arbitrary",) * 3,
                                  vmem_limit_bytes=VMEM_LIMIT)
    return grid, q_spec, kv_spec, out_spec, out_shape, scratch, params


def _mla_attn(q, k, v, t):
    nb, _, seq, _ = q.shape
    tq, tk = t["tq_mla"], t["tk"]
    grid, q_spec, kv_spec, out_spec, out_shape, scratch, params = _attn_common(nb, seq, tq, tk, tq)
    return pl.pallas_call(
        functools.partial(_mla_attn_kernel, seq=seq, tq=tq, tk=tk, vchunk=t["vchunk"]),
        grid=grid, in_specs=[q_spec, kv_spec, kv_spec], out_specs=out_spec, out_shape=out_shape,
        scratch_shapes=scratch, compiler_params=params, name="mla_attn",
    )(q, k, v)


def _diff_attn(q, k, v, bias_tiles, cfar, lam_params, subln_g, lam_init, t):
    nb, _, seq, _ = q.shape
    tq, tk = t["tq_diff"], t["tk"]
    grid, q_spec, kv_spec, out_spec, out_shape, scratch, params = _attn_common(nb, seq, tq, tk, 2 * tq)
    return pl.pallas_call(
        functools.partial(_diff_attn_kernel, seq=seq, tq=tq, tk=tk, vchunk=t["vchunk"],
                          lam_init=lam_init),
        grid=grid,
        in_specs=[q_spec, kv_spec, kv_spec,
                  pl.BlockSpec((1, 3, tk, tk), lambda b, h, i: (h, 0, 0, 0)),
                  pl.BlockSpec(memory_space=pltpu.SMEM),
                  pl.BlockSpec((8, LANES), lambda b, h, i: (0, 0)),
                  pl.BlockSpec((1, C_V), lambda b, h, i: (0, 0))],
        out_specs=out_spec, out_shape=out_shape, scratch_shapes=scratch, compiler_params=params,
        name="diff_attn",
    )(q, k, v, bias_tiles, cfar, lam_params, subln_g)


def _bias_kernel(bucket_ref, rb_ref, o_ref):
    hd = pl.program_id(0)
    bucket = bucket_ref[0]
    acc = jnp.zeros(bucket.shape, F32)
    for b in range(REL_BUCKETS):
        acc = jnp.where(bucket == b, rb_ref[b, hd], acc)
    o_ref[0, 0] = acc * LOG2E


def _bias_tiles(bucket_t, rel_bias):
    n_tiles, tk, tq = bucket_t.shape
    return pl.pallas_call(
        _bias_kernel,
        grid=(C_HEADS, n_tiles),
        in_specs=[pl.BlockSpec((1, tk, tq), lambda h, i: (i, 0, 0)),
                  pl.BlockSpec(memory_space=pltpu.SMEM)],
        out_specs=pl.BlockSpec((1, 1, tk, tq), lambda h, i: (h, i, 0, 0)),
        out_shape=jax.ShapeDtypeStruct((C_HEADS, n_tiles, tk, tq), F32),
        name="bias_tiles",
    )(bucket_t, rel_bias)


def _post_kernel(x_ref, mixa_ref, ob_ref, oc_ref, wout_ref, mod_ref, n2g_ref, rw_ref, rb_ref,
                 x2_ref, h2_ref, idx_ref, gate_ref, rank_ref, cnt_ref, mix_scr, base_scr, *, tm):
    @pl.when((pl.program_id(0) == 0) & (pl.program_id(1) == 0))
    def _():
        base_scr[...] = jnp.zeros(base_scr.shape, F32)

    mix_scr[:, 0:A_WIDTH] = mixa_ref[0]
    for hd in range(B_HEADS):
        mix_scr[:, A_WIDTH + hd * B_V:A_WIDTH + (hd + 1) * B_V] = ob_ref[0, hd]
    off_c = A_WIDTH + B_HEADS * B_V
    for hd in range(C_HEADS):
        mix_scr[:, off_c + hd * C_V:off_c + (hd + 1) * C_V] = oc_ref[0, hd]
    y = jnp.dot(mix_scr[...], wout_ref[...], preferred_element_type=F32)
    mod = mod_ref[0]
    x2 = x_ref[0] + mod[2:3] * y
    x2_ref[0] = x2
    h2 = _rms(x2, n2g_ref[...]) * (1.0 + mod[4:5]) + mod[3:4]
    h2_ref[0] = h2

    logits = jnp.dot(h2, rw_ref[...], precision=HIGHEST, preferred_element_type=F32) + rb_ref[...]
    lane = lax.broadcasted_iota(jnp.int32, (tm, LANES), 1)
    vals, idxs = [], []
    cur = logits
    for _ in range(TOP_K):
        mx = jnp.max(cur, axis=1, keepdims=True)
        ik = jnp.min(jnp.where(cur == mx, lane, LANES), axis=1, keepdims=True)
        vals.append(mx)
        idxs.append(ik)
        cur = jnp.where(lane == ik, NEG_BIG, cur)
    ex = [jnp.exp(vk - vals[0]) for vk in vals]
    den = ex[0] + ex[1] + ex[2] + ex[3]

    onehots = [(lane == ik).astype(F32) for ik in idxs]
    esum = onehots[0] + onehots[1] + onehots[2] + onehots[3]
    r_i = lax.broadcasted_iota(jnp.int32, (tm, tm), 0)
    c_i = lax.broadcasted_iota(jnp.int32, (tm, tm), 1)
    lower = jnp.where(r_i > c_i, 1.0, 0.0).astype(BF16)
    before = base_scr[...] + jnp.dot(lower, esum.astype(BF16), preferred_element_type=F32)
    idx_out = jnp.zeros((tm, LANES), jnp.int32)
    gate_out = jnp.zeros((tm, LANES), F32)
    rank_out = jnp.zeros((tm, LANES), jnp.int32)
    for k in range(TOP_K):
        rk = jnp.sum(onehots[k] * before, axis=1, keepdims=True).astype(jnp.int32)
        idx_out = jnp.where(lane == k, idxs[k], idx_out)
        gate_out = jnp.where(lane == k, ex[k] / den, gate_out)
        rank_out = jnp.where(lane == k, rk, rank_out)
    idx_ref[0] = idx_out
    gate_ref[0] = gate_out
    rank_ref[0] = rank_out
    base_new = base_scr[...] + jnp.sum(esum, axis=0, keepdims=True)
    base_scr[...] = base_new
    cnt_ref[...] = base_new


def _post(x, mixa, ob, oc, mod_l, lw, t):
    nb, seq, _ = x.shape
    tm = t["tm"]
    full = lambda a: pl.BlockSpec(a.shape, lambda b, i: (0,) * a.ndim)
    tok = lambda w: pl.BlockSpec((1, tm, w), lambda b, i: (b, i, 0))
    head_in = pl.BlockSpec((1, B_HEADS, tm, B_V), lambda b, i: (b, 0, i, 0))
    return pl.pallas_call(
        functools.partial(_post_kernel, tm=tm),
        grid=(nb, seq // tm),
        in_specs=[tok(D_MODEL), tok(A_WIDTH), head_in, head_in, full(lw["wout"]),
                  pl.BlockSpec((1, 6, D_MODEL), lambda b, i: (b, 0, 0)),
                  full(lw["n2g"]), full(lw["rw"]), full(lw["rb"])],
        out_specs=[tok(D_MODEL), tok(D_MODEL), tok(LANES), tok(LANES), tok(LANES),
                   pl.BlockSpec((1, LANES), lambda b, i: (0, 0))],
        out_shape=[jax.ShapeDtypeStruct((nb, seq, D_MODEL), F32),
                   jax.ShapeDtypeStruct((nb, seq, D_MODEL), F32),
                   jax.ShapeDtypeStruct((nb, seq, LANES), jnp.int32),
                   jax.ShapeDtypeStruct((nb, seq, LANES), F32),
                   jax.ShapeDtypeStruct((nb, seq, LANES), jnp.int32),
                   jax.ShapeDtypeStruct((1, LANES), F32)],
        scratch_shapes=[pltpu.VMEM((tm, D_MODEL), BF16), pltpu.VMEM((1, LANES), F32)],
        compiler_params=pltpu.CompilerParams(dimension_semantics=("arbitrary", "arbitrary"),
                                             vmem_limit_bytes=VMEM_LIMIT),
        name="post",
    )(x, mixa, ob, oc, lw["wout"], mod_l, lw["n2g"], lw["rw"], lw["rb"])


def _row_copy(src, src_row, dst, dst_row, sem):
    return pltpu.make_async_copy(src.at[pl.ds(src_row, 1)], dst.at[pl.ds(dst_row, 1)], sem)


def _dispatch_kernel(pos_ref, h2_ref, xs_ref, sem, *, td):
    def start(r, carry):
        for k in range(TOP_K):
            _row_copy(h2_ref, r, xs_ref, pos_ref[0, 0, r * TOP_K + k], sem).start(priority=k % 2)
        return carry

    def wait(r, carry):
        for k in range(TOP_K):
            _row_copy(h2_ref, 0, xs_ref, 0, sem).wait()
        return carry

    lax.fori_loop(0, td, start, 0)
    lax.fori_loop(0, td, wait, 0)


def _dispatch(pos, h2, n_slots, t):
    n_tok = h2.shape[0]
    td = t["td"]
    return pl.pallas_call(
        functools.partial(_dispatch_kernel, td=td),
        grid=(n_tok // td,),
        in_specs=[pl.BlockSpec((1, 1, td * TOP_K), lambda i: (i, 0, 0), memory_space=pltpu.SMEM),
                  pl.BlockSpec((td, D_MODEL), lambda i: (i, 0))],
        out_specs=pl.BlockSpec(memory_space=pl.ANY),
        out_shape=jax.ShapeDtypeStruct((n_slots, D_MODEL), F32),
        scratch_shapes=[pltpu.SemaphoreType.DMA(())],
        compiler_params=pltpu.CompilerParams(dimension_semantics=("arbitrary",)),
        name="dispatch",
    )(pos.reshape(n_tok // td, 1, td * TOP_K), h2)


def _expert_kernel(blk_e_ref, valid_ref, first_ref, xs_ref, w1_ref, b1_ref, w2_ref, b2_ref, y_ref,
                   w1_scr, w2_scr, *, tmo):
    del blk_e_ref
    i = pl.program_id(0)

    @pl.when(first_ref[i] == 1)
    def _():
        w1_scr[...] = w1_ref[0].astype(BF16)
        w2_scr[...] = w2_ref[0].astype(BF16)

    @pl.when(valid_ref[i] > 0)
    def _():
        row = lax.broadcasted_iota(jnp.int32, (tmo, 1), 0)
        x = jnp.where(row < valid_ref[i], xs_ref[...], 0.0).astype(BF16)
        hmid = jnp.dot(x, w1_scr[...], preferred_element_type=F32) + b1_ref[0]
        gate = jnp.minimum(hmid[:, 0:D_FF], SWIGLU_LIMIT)
        up = jnp.clip(hmid[:, D_FF:2 * D_FF], -SWIGLU_LIMIT, SWIGLU_LIMIT)
        act = (up + 1.0) * (gate * jax.nn.sigmoid(SWIGLU_ALPHA * gate))
        y_ref[...] = jnp.dot(act.astype(BF16), w2_scr[...], preferred_element_type=F32) + b2_ref[0]

    @pl.when(valid_ref[i] == 0)
    def _():
        y_ref[...] = jnp.zeros(y_ref.shape, F32)


def _experts(blk_e, valid, first, xs, w1, b1, w2, b2, layer, t):
    n_slots = xs.shape[0]
    tmo = t["tmo"]
    off = layer * N_EXPERTS
    grid_spec = pltpu.PrefetchScalarGridSpec(
        num_scalar_prefetch=3,
        grid=(n_slots // tmo,),
        in_specs=[pl.BlockSpec((tmo, D_MODEL), lambda i, e, v, f: (i, 0)),
                  pl.BlockSpec((1, D_MODEL, 2 * D_FF), lambda i, e, v, f: (off + e[i], 0, 0)),
                  pl.BlockSpec((1, 1, 2 * D_FF), lambda i, e, v, f: (off + e[i], 0, 0)),
                  pl.BlockSpec((1, D_FF, D_MODEL), lambda i, e, v, f: (off + e[i], 0, 0)),
                  pl.BlockSpec((1, 1, D_MODEL), lambda i, e, v, f: (off + e[i], 0, 0))],
        out_specs=pl.BlockSpec((tmo, D_MODEL), lambda i, e, v, f: (i, 0)),
        scratch_shapes=[pltpu.VMEM((D_MODEL, 2 * D_FF), BF16), pltpu.VMEM((D_FF, D_MODEL), BF16)],
    )
    return pl.pallas_call(
        functools.partial(_expert_kernel, tmo=tmo),
        grid_spec=grid_spec,
        out_shape=jax.ShapeDtypeStruct((n_slots, D_MODEL), F32),
        compiler_params=pltpu.CompilerParams(dimension_semantics=("arbitrary",),
                                             vmem_limit_bytes=VMEM_LIMIT),
        name="experts",
    )(blk_e, valid, first, xs, w1, b1, w2, b2)


def _combine_kernel(pos_ref, gate_ref, x2_ref, mod_ref, fg_ref, y_ref, o_ref, buf, sem, *, tc, final):
    def start(r, carry):
        for k in range(TOP_K):
            _row_copy(y_ref, pos_ref[0, 0, r * TOP_K + k], buf.at[k], r, sem).start(priority=k % 2)
        return carry

    def wait(r, carry):
        for k in range(TOP_K):
            _row_copy(y_ref, 0, buf.at[k], 0, sem).wait()
        return carry

    lax.fori_loop(0, tc, start, 0)
    lax.fori_loop(0, tc, wait, 0)
    gates = gate_ref[...]
    moe = buf[0] * gates[:, 0:1]
    for k in range(1, TOP_K):
        moe = moe + buf[k] * gates[:, k:k + 1]
    out = x2_ref[...] + mod_ref[0][5:6] * moe
    if final:
        out = _rms(out, fg_ref[...])
    o_ref[...] = out


def _combine(pos, gates, x2, mod_l, final_g, y, seq, t, final, tok_range):
    n_tok = x2.shape[0]
    tc = t["tc"]
    per_b = seq // tc
    first_tile = tok_range[0] // tc
    n_out = tok_range[1] - tok_range[0]
    return pl.pallas_call(
        functools.partial(_combine_kernel, tc=tc, final=final),
        grid=(n_out // tc,),
        in_specs=[pl.BlockSpec((1, 1, tc * TOP_K), lambda i: (i + first_tile, 0, 0),
                               memory_space=pltpu.SMEM),
                  pl.BlockSpec((tc, LANES), lambda i: (i + first_tile, 0)),
                  pl.BlockSpec((tc, D_MODEL), lambda i: (i + first_tile, 0)),
                  pl.BlockSpec((1, 6, D_MODEL), lambda i: ((i + first_tile) // per_b, 0, 0)),
                  pl.BlockSpec((1, D_MODEL), lambda i: (0, 0)),
                  pl.BlockSpec(memory_space=pl.ANY)],
        out_specs=pl.BlockSpec((tc, D_MODEL), lambda i: (i, 0)),
        out_shape=jax.ShapeDtypeStruct((n_out, D_MODEL), F32),
        scratch_shapes=[pltpu.VMEM((TOP_K, tc, D_MODEL), F32), pltpu.SemaphoreType.DMA(())],
        compiler_params=pltpu.CompilerParams(dimension_semantics=("arbitrary",),
                                             vmem_limit_bytes=VMEM_LIMIT),
        name="combine",
    )(pos.reshape(n_tok // tc, 1, tc * TOP_K), gates, x2, mod_l, final_g, y)


def _rot_half_cols(w):
    half = w.shape[-1] // 2
    return jnp.concatenate([-w[..., half:], w[..., :half]], -1)


def _pad_heads(w, n_heads, width):
    lead = w.shape[:-1]
    w = w.reshape(lead + (n_heads, width))
    w = jnp.pad(w, [(0, 0)] * len(lead) + [(0, 0), (0, HEAD_PAD - width)])
    return w.reshape(lead + (n_heads * HEAD_PAD,))


def _rope_block(w):
    return jnp.pad(w, [(0, 0)] * (w.ndim - 1) + [(B_NOPE, HEAD_PAD - B_NOPE - B_ROPE)])


def _pack_layer_weights(w_in, mla_w_uq, mla_w_ukv):
    o = [0, 256, 512, 896, 1152, 1184, 1568, 1952, 2336]
    a = w_in[..., o[0]:o[2]]
    cq = w_in[..., o[2]:o[3]]
    ckv = w_in[..., o[3]:o[4]]
    kr = w_in[..., o[4]:o[5]]
    dq, dk, dv = (w_in[..., o[5]:o[6]], w_in[..., o[6]:o[7]], w_in[..., o[7]:o[8]])
    win = jnp.concatenate([a, cq, ckv, _rope_block(kr), _rope_block(_rot_half_cols(kr)),
                           _pad_heads(dq, C_HEADS, 2 * C_QK), _pad_heads(dk, C_HEADS, 2 * C_QK),
                           _pad_heads(dv, C_HEADS, C_V)], -1).astype(BF16)
    lead = mla_w_uq.shape[:-1]
    uq = mla_w_uq.reshape(lead + (B_HEADS, B_NOPE + B_ROPE))
    nope, rp = uq[..., :B_NOPE], uq[..., B_NOPE:]
    zeros32 = jnp.zeros_like(rp)
    wqa = jnp.concatenate([nope, rp, zeros32], -1).reshape(lead + (B_HEADS * HEAD_PAD,))
    wqb = jnp.concatenate([jnp.zeros_like(nope), _rot_half_cols(rp), zeros32], -1)
    wqb = wqb.reshape(lead + (B_HEADS * HEAD_PAD,))
    lead = mla_w_ukv.shape[:-1]
    ukv = mla_w_ukv.reshape(lead + (B_HEADS, B_NOPE + B_V))
    kn, vv = ukv[..., :B_NOPE], ukv[..., B_NOPE:]
    wkn = jnp.concatenate([kn, jnp.zeros_like(kn)], -1).reshape(lead + (B_HEADS * HEAD_PAD,))
    wv = jnp.concatenate([vv, jnp.zeros_like(vv)], -1).reshape(lead + (B_HEADS * HEAD_PAD,))
    return win, wqa.astype(BF16), wqb.astype(BF16), wkn.astype(BF16), wv.astype(BF16)


def _rope_tables(seq):
    pos = jnp.arange(seq, dtype=F32)
    inv = 1.0 / (ROPE_BASE ** (jnp.arange(B_ROPE // 2, dtype=F32) / (B_ROPE // 2)))
    ang = pos[:, None] * inv[None, :]
    cos = jnp.concatenate([jnp.cos(ang)] * 2, -1)
    sin = jnp.concatenate([jnp.sin(ang)] * 2, -1)
    qscale = (B_NOPE + B_ROPE) ** -0.5 * LOG2E
    ones = jnp.ones((seq, B_NOPE), F32)
    pad = jnp.zeros((seq, HEAD_PAD - B_NOPE - B_ROPE), F32)
    return dict(cosq=jnp.concatenate([ones, cos, pad], -1) * qscale,
                sinq=_rope_block(sin) * qscale,
                cosk=_rope_block(cos), sink=_rope_block(sin))


def _t5_bucket(rel):
    half = REL_BUCKETS // 2
    max_exact = half // 2
    base = jnp.where(rel > 0, half, 0)
    n = jnp.abs(rel)
    nf = jnp.maximum(n, 1).astype(F32)
    large = max_exact + (jnp.log(nf / max_exact) / math.log(REL_MAX_DIST / max_exact)
                         * (half - max_exact)).astype(jnp.int32)
    large = jnp.minimum(large, half - 1)
    return base + jnp.where(n < max_exact, n, large)


def _band_buckets(tk):
    kk = jnp.arange(tk)[:, None]
    qq = jnp.arange(tk)[None, :]
    return jnp.stack([_t5_bucket(e * tk + kk - qq) for e in (-1, 0, 1)], 0)


def kernel(x_prompt, x_sample, c_prompt, c_sample, ada_w, ada_b, norm1_g, w_in, gmlp_ln_g, gmlp_ln_b,
           gmlp_ws, gmlp_bs, mla_q_norm_g, mla_w_uq, mla_kv_norm_g, mla_w_ukv, diff_lq1, diff_lk1,
           diff_lq2, diff_lk2, diff_subln_g, w_out, norm2_g, router_w, router_b, moe_w1, moe_b1,
           moe_w2, moe_b2, rel_bias, final_g):
    depth = ada_w.shape[0]
    n_prompt = x_prompt.shape[0]
    assert x_prompt.shape[1] == x_sample.shape[1]
    x = jnp.concatenate([x_prompt, x_sample], 0)
    nb, seq, _ = x.shape
    n_tok = nb * seq
    t = _tiles(seq)
    assert t["tk"] >= REL_MAX_DIST and t["tq_diff"] % t["tk"] == 0 and seq // t["tk"] >= 2

    c = jnp.concatenate([c_prompt, c_sample], 0)
    c_pad = jnp.pad(c, ((0, 8 - nb), (0, 0)))
    mod = _adaln(c_pad, ada_w, ada_b)[:, :nb].reshape(depth, nb, 6, D_MODEL)

    win, wqa, wqb, wkn, wv = _pack_layer_weights(w_in, mla_w_uq, mla_w_ukv)
    ws_stack = gmlp_ws.reshape(depth, A_GROUPS * GMLP_CHUNK, GMLP_CHUNK).astype(BF16)
    bs_tile = jnp.repeat(jnp.swapaxes(gmlp_bs, 1, 2), A_WIDTH // A_GROUPS, axis=2)
    wout16 = w_out.astype(BF16)
    rw_pad = jnp.pad(router_w, ((0, 0), (0, 0), (0, LANES - N_EXPERTS)))
    rb_pad = jnp.pad(router_b, ((0, 0), (0, LANES - N_EXPERTS)), constant_values=NEG_BIG)
    w1_all = moe_w1.reshape(depth * N_EXPERTS, D_MODEL, 2 * D_FF)
    b1_all = moe_b1.reshape(depth * N_EXPERTS, 1, 2 * D_FF)
    w2_all = moe_w2.reshape(depth * N_EXPERTS, D_FF, D_MODEL)
    b2_all = moe_b2.reshape(depth * N_EXPERTS, 1, D_MODEL)
    tabs = _rope_tables(seq)
    bias_tiles = _bias_tiles(_band_buckets(t["tk"]).astype(jnp.int32), rel_bias)
    half = REL_BUCKETS // 2
    cfar = jnp.stack([rel_bias[half - 1], rel_bias[REL_BUCKETS - 1]], -1) * LOG2E
    lam_rows = jnp.stack([diff_lq1, diff_lk1, diff_lq2, diff_lk2], 1)
    lam_rows = jnp.pad(lam_rows, ((0, 0), (0, 4), (0, LANES - C_QK)))

    tmo = t["tmo"]
    n_tiles = -(-(n_tok * TOP_K) // tmo) + N_EXPERTS
    n_slots = n_tiles * tmo
    final_row = final_g.reshape(1, D_MODEL)

    for l in range(depth):
        lw = dict(n1g=norm1_g[l][None], win=win[l], lng=gmlp_ln_g[l][None], lnb=gmlp_ln_b[l][None],
                  ws=ws_stack[l], bs=bs_tile[l], qng=mla_q_norm_g[l][None], wqa=wqa[l], wqb=wqb[l],
                  kvng=mla_kv_norm_g[l][None], wkn=wkn[l], wv=wv[l], wout=wout16[l],
                  n2g=norm2_g[l][None], rw=rw_pad[l], rb=rb_pad[l][None])
        lam_init = 0.8 - 0.6 * math.exp(-0.3 * l)
        mixa, qb, kb, vb, qd, kd, vd = _pre(x, mod[l], lw, tabs, t)
        ob = _mla_attn(qb, kb, vb, t)
        oc = _diff_attn(qd, kd, vd, bias_tiles, cfar, lam_rows[l], diff_subln_g[l][None], lam_init, t)
        x2, h2, idx, gates, rank, cnt = _post(x, mixa, ob, oc, mod[l], lw, t)

        counts = cnt[0, :N_EXPERTS].astype(jnp.int32)
        padded = ((counts + tmo - 1) // tmo) * tmo
        pend = jnp.cumsum(padded)
        pstart = pend - padded
        idx4 = idx.reshape(n_tok, LANES)[:, :TOP_K]
        pos = (jnp.take(pstart, idx4) + rank.reshape(n_tok, LANES)[:, :TOP_K]).reshape(-1)
        tile_start = jnp.arange(n_tiles, dtype=jnp.int32) * tmo
        blk_e = jnp.minimum(jnp.sum(tile_start[:, None] >= pend[None, :], axis=1),
                            N_EXPERTS - 1).astype(jnp.int32)
        valid = jnp.clip(jnp.take(pstart + counts, blk_e) - tile_start, 0, tmo).astype(jnp.int32)
        first = jnp.concatenate([jnp.ones((1,), jnp.int32),
                                 (blk_e[1:] != blk_e[:-1]).astype(jnp.int32)])

        xs = _dispatch(pos, h2.reshape(n_tok, D_MODEL), n_slots, t)
        y = _experts(blk_e, valid, first, xs, w1_all, b1_all, w2_all, b2_all, l, t)
        comb = functools.partial(_combine, pos, gates.reshape(n_tok, LANES), x2.reshape(n_tok, D_MODEL),
                                 mod[l], final_row, y, seq, t)
        if l < depth - 1:
            x = comb(False, (0, n_tok)).reshape(nb, seq, D_MODEL)
    n_p = n_prompt * seq
    return (comb(True, (0, n_p)).reshape(n_prompt, seq, D_MODEL),
            comb(True, (n_p, n_tok)).reshape(nb - n_prompt, seq, D_MODEL))
```

```python
import functools
import math

import jax
import jax.numpy as jnp
from jax import lax
from jax.experimental import pallas as pl
from jax.experimental.pallas import tpu as pltpu

F32 = jnp.float32
BF16 = jnp.bfloat16
HIGHEST = lax.Precision.HIGHEST
LOG2E = 1.4426950408889634

D_MODEL = 1024
EPS = 1e-6
GMLP_CHUNK = 128
A_GROUPS = 4
A_WIDTH = 256
B_HEADS = 6
B_Q_LORA = 384
B_KV_LORA = 256
B_NOPE = 64
B_ROPE = 32
B_V = 64
ROPE_BASE = 10000.0
C_HEADS = 6
C_QK = 32
C_V = 64
REL_BUCKETS = 32
REL_MAX_DIST = 128
N_EXPERTS = 32
TOP_K = 4
D_FF = 1024
SWIGLU_LIMIT = 7.0
SWIGLU_ALPHA = 1.702

LANES = 128
HEAD_PAD = 128
ONES_COL = 64
PV_ROWS = 80
NEG_BIG = -3.0e38
VMEM_LIMIT = 56 * 1024 * 1024

SEG_A = (0, 512)
SEG_CQ = (512, 896)
SEG_CKV = (896, 1152)
SEG_KR = (1152, 1280)
SEG_KROT = (1280, 1408)
SEG_DQ = (1408, 2176)
SEG_DK = (2176, 2944)
SEG_DV = (2944, 3712)
IN_PACKED = 3712


def _tiles(seq):
    return dict(
        tm=min(512, seq),
        tq_mla=min(8192, seq),
        tq_diff=min(4096, seq),
        tk=min(256, seq),
        vchunk=min(512, seq),
        td=min(256, seq),
        tmo=512,
        tc=min(256, seq),
    )


def _rms(x, g):
    return x * lax.rsqrt(jnp.mean(x * x, -1, keepdims=True) + EPS) * g


def _adaln_kernel(c_ref, w_ref, b_ref, o_ref):
    c = c_ref[...]
    cs = c * jax.nn.sigmoid(c)
    o_ref[0] = jnp.dot(cs, w_ref[0], precision=HIGHEST, preferred_element_type=F32) + b_ref[0]


def _adaln(c_pad, ada_w, ada_b):
    depth = ada_w.shape[0]
    nb = 6
    return pl.pallas_call(
        _adaln_kernel,
        grid=(depth, nb),
        in_specs=[
            pl.BlockSpec((8, D_MODEL), lambda l, j: (0, 0)),
            pl.BlockSpec((1, D_MODEL, D_MODEL), lambda l, j: (l, 0, j)),
            pl.BlockSpec((1, 1, D_MODEL), lambda l, j: (l, 0, j)),
        ],
        out_specs=pl.BlockSpec((1, 8, D_MODEL), lambda l, j: (l, 0, j)),
        out_shape=jax.ShapeDtypeStruct((depth, 8, 6 * D_MODEL), F32),
        compiler_params=pltpu.CompilerParams(dimension_semantics=("arbitrary", "arbitrary")),
        name="adaln",
    )(c_pad, ada_w, ada_b.reshape(depth, 1, 6 * D_MODEL))


def _pre_kernel(x_ref, mod_ref, n1g_ref, win_ref, lng_ref, lnb_ref, ws_ref, bs_ref,
                qng_ref, wqa_ref, wqb_ref, kvng_ref, wkn_ref, wv_ref,
                cosq_ref, sinq_ref, cosk_ref, sink_ref,
                mixa_ref, qb_ref, kb_ref, vb_ref, qd_ref, kd_ref, vd_ref, *, tm):
    x = x_ref[0]
    mod = mod_ref[0]
    h = (_rms(x, n1g_ref[...]) * (1.0 + mod[1:2]) + mod[0:1]).astype(BF16)

    def seg(s):
        return jnp.dot(h, win_ref[:, s[0]:s[1]], preferred_element_type=F32)

    za = seg(SEG_A)
    u = jax.nn.gelu(za[:, 0:A_WIDTH])
    v = jax.nn.gelu(za[:, A_WIDTH:2 * A_WIDTH])
    vc = v - jnp.mean(v, -1, keepdims=True)
    v = vc * lax.rsqrt(jnp.mean(vc * vc, -1, keepdims=True) + EPS) * lng_ref[...] + lnb_ref[...]
    v16 = v.astype(BF16)
    grp = lax.broadcasted_iota(jnp.int32, (1, A_WIDTH), 1) // (A_WIDTH // A_GROUPS)
    for n in range(tm // GMLP_CHUNK):
        rows = slice(n * GMLP_CHUNK, (n + 1) * GMLP_CHUNK)
        r = jnp.dot(ws_ref[...], v16[rows], preferred_element_type=F32)
        sv = bs_ref[...]
        for g in range(A_GROUPS):
            sv = sv + jnp.where(grp == g, r[g * GMLP_CHUNK:(g + 1) * GMLP_CHUNK], 0.0)
        mixa_ref[0, rows, :] = (u[rows] * sv).astype(BF16)

    lane = lax.broadcasted_iota(jnp.int32, (1, HEAD_PAD), 1)
    ones_col = jnp.where(lane == ONES_COL, 1.0, 0.0).astype(F32)

    cqn = _rms(seg(SEG_CQ), qng_ref[...]).astype(BF16)
    qa = jnp.dot(cqn, wqa_ref[...], preferred_element_type=F32)
    qb = jnp.dot(cqn, wqb_ref[...], preferred_element_type=F32)
    cosq = cosq_ref[...]
    sinq = sinq_ref[...]
    for hd in range(B_HEADS):
        cols = slice(hd * HEAD_PAD, (hd + 1) * HEAD_PAD)
        qb_ref[0, hd] = (qa[:, cols] * cosq + qb[:, cols] * sinq).astype(BF16)
    ckvn = _rms(seg(SEG_CKV), kvng_ref[...]).astype(BF16)
    kn = jnp.dot(ckvn, wkn_ref[...], preferred_element_type=F32)
    vv = jnp.dot(ckvn, wv_ref[...], preferred_element_type=F32)
    krp = seg(SEG_KR) * cosk_ref[...] + seg(SEG_KROT) * sink_ref[...]
    for hd in range(B_HEADS):
        cols = slice(hd * HEAD_PAD, (hd + 1) * HEAD_PAD)
        kb_ref[0, hd] = (kn[:, cols] + krp).astype(BF16)
        vb_ref[0, hd] = (vv[:, cols] + ones_col).astype(BF16)

    zdq = seg(SEG_DQ) * (C_QK ** -0.5 * LOG2E)
    zdk = seg(SEG_DK)
    zdv = seg(SEG_DV)
    for hd in range(C_HEADS):
        cols = slice(hd * HEAD_PAD, (hd + 1) * HEAD_PAD)
        qd_ref[0, hd] = zdq[:, cols].astype(BF16)
        kd_ref[0, hd] = zdk[:, cols].astype(BF16)
        vd_ref[0, hd] = (zdv[:, cols] + ones_col).astype(BF16)


def _pre(x, mod_l, lw, tabs, t):
    nb, seq, _ = x.shape
    tm = t["tm"]
    full = lambda a: pl.BlockSpec(a.shape, lambda b, i: (0,) * a.ndim)
    tab = pl.BlockSpec((tm, HEAD_PAD), lambda b, i: (i, 0))
    head_out = pl.BlockSpec((1, B_HEADS, tm, HEAD_PAD), lambda b, i: (b, 0, i, 0))
    head_shape = jax.ShapeDtypeStruct((nb, B_HEADS, seq, HEAD_PAD), BF16)
    consts = [lw["n1g"], lw["win"], lw["lng"], lw["lnb"], lw["ws"], lw["bs"],
              lw["qng"], lw["wqa"], lw["wqb"], lw["kvng"], lw["wkn"], lw["wv"]]
    return pl.pallas_call(
        functools.partial(_pre_kernel, tm=tm),
        grid=(nb, seq // tm),
        in_specs=[pl.BlockSpec((1, tm, D_MODEL), lambda b, i: (b, i, 0)),
                  pl.BlockSpec((1, 6, D_MODEL), lambda b, i: (b, 0, 0))]
                 + [full(a) for a in consts] + [tab] * 4,
        out_specs=[pl.BlockSpec((1, tm, A_WIDTH), lambda b, i: (b, i, 0))] + [head_out] * 6,
        out_shape=[jax.ShapeDtypeStruct((nb, seq, A_WIDTH), BF16)] + [head_shape] * 6,
        compiler_params=pltpu.CompilerParams(dimension_semantics=("arbitrary", "arbitrary"),
                                             vmem_limit_bytes=VMEM_LIMIT),
        name="pre",
    )(x, mod_l, *consts, tabs["cosq"], tabs["sinq"], tabs["cosk"], tabs["sink"])


def _transpose_v(v_ref, vt_ref, *, seq, vchunk, tk):
    per = vchunk // tk

    def body(i, carry):
        blk = v_ref[0, 0, pl.ds(pl.multiple_of(i * vchunk, vchunk), vchunk), :]
        blk_t = blk.astype(F32).T.astype(BF16)
        for c in range(per):
            vt_ref[i * per + c] = blk_t[:, c * tk:(c + 1) * tk]
        return carry

    lax.fori_loop(0, seq // vchunk, body, 0)


def _attn_pipeline(k_ref, vt_ref, qt_ref, scr, run_steps, *, nk, tk, use_shift):
    m_ref, acc_ref, s_buf, cm_buf, c_buf, p_buf, a_buf = scr
    m_ref[...] = jnp.full(m_ref.shape, NEG_BIG, F32)
    acc_ref[...] = jnp.zeros(acc_ref.shape, F32)
    s_buf[...] = jnp.full(s_buf.shape, -jnp.inf, F32)
    cm_buf[...] = jnp.full(cm_buf.shape, NEG_BIG, F32)
    c_buf[...] = jnp.zeros(c_buf.shape, F32)
    p_buf[...] = jnp.zeros(p_buf.shape, BF16)
    a_buf[...] = jnp.ones(a_buf.shape, F32)

    def stage_a(t, shift, bias):
        kt = k_ref[0, 0, pl.ds(pl.multiple_of(t * tk, tk), tk), :]
        s = jnp.dot(kt, qt_ref[...], preferred_element_type=F32)
        if bias:
            cols = [s[:, c * tk:(c + 1) * tk] for c in range(s.shape[1] // tk)]
            for c, tile in bias:
                cols[c] = cols[c] + tile
            s = jnp.concatenate(cols, axis=1)
        s_buf[...] = s
        cm = jnp.max(s, axis=0, keepdims=True)
        if use_shift:
            c = jnp.broadcast_to(jnp.asarray(shift, F32), c_buf.shape)
            c_buf[...] = c
            cm = cm + c
        cm_buf[...] = cm

    def stage_b():
        m_old = m_ref[...]
        m_new = jnp.maximum(m_old, cm_buf[...])
        ref = m_new - c_buf[...] if use_shift else m_new
        p_buf[...] = jnp.exp2(s_buf[...] - ref).astype(BF16)
        a_buf[...] = jnp.exp2(m_old - m_new)
        m_ref[...] = m_new

    def stage_c(t):
        pv = jnp.dot(vt_ref[t][0:PV_ROWS], p_buf[...], preferred_element_type=F32)
        acc_ref[...] = acc_ref[...] * a_buf[...] + pv

    def step(t, shift=0.0, bias=None, do_a=True, do_b=True):
        stage_c(jnp.maximum(t - 2, 0))
        if do_b:
            stage_b()
        if do_a:
            stage_a(t, shift, bias)

    run_steps(step)
    step(nk, do_a=False)
    step(nk + 1, do_a=False, do_b=False)


def _mla_attn_kernel(q_ref, k_ref, v_ref, o_ref, vt_ref, qt_ref, *scr, seq, tq, tk, vchunk):
    @pl.when(pl.program_id(2) == 0)
    def _():
        _transpose_v(v_ref, vt_ref, seq=seq, vchunk=vchunk, tk=tk)

    qt_ref[...] = q_ref[0, 0].astype(F32).T.astype(BF16)
    nk = seq // tk

    def run_steps(step):
        def body(t, carry):
            step(t)
            return carry
        lax.fori_loop(0, nk, body, 0)

    _attn_pipeline(k_ref, vt_ref, qt_ref, scr, run_steps, nk=nk, tk=tk, use_shift=False)
    acc = scr[1][...]
    o = acc[0:B_V] / acc[ONES_COL:ONES_COL + 1]
    o_ref[0, 0] = o.T.astype(BF16)


def _diff_attn_kernel(q_ref, k_ref, v_ref, bias_ref, cfar_ref, lamp_ref, g_ref, o_ref, vt_ref, qt_ref,
                      *scr, seq, tq, tk, vchunk, lam_init):
    hd = pl.program_id(1)
    qi = pl.program_id(2)

    @pl.when(qi == 0)
    def _():
        _transpose_v(v_ref, vt_ref, seq=seq, vchunk=vchunk, tk=tk)

    qt = q_ref[0, 0].astype(F32).T
    row = lax.broadcasted_iota(jnp.int32, (HEAD_PAD, 1), 0)
    qt_ref[:, 0:tq] = jnp.where(row < C_QK, qt, 0.0).astype(BF16)
    qt_ref[:, tq:2 * tq] = jnp.where(row < C_QK, 0.0, qt).astype(BF16)

    nk = seq // tk
    chunks = tq // tk
    n_band = chunks + 2
    j_lo = qi * chunks - 1
    c_left = cfar_ref[hd, 0]
    c_right = cfar_ref[hd, 1]

    def band_bias(i):
        shifts, parts = [], []
        for c in range(chunks):
            e = i - 1 - c
            if e <= -2:
                shifts.append(jnp.full((1, tk), c_left, F32))
            elif e >= 2:
                shifts.append(jnp.full((1, tk), c_right, F32))
            else:
                shifts.append(jnp.zeros((1, tk), F32))
                parts += [(c, bias_ref[0, e + 1]), (chunks + c, bias_ref[0, e + 1])]
        return jnp.concatenate(shifts + shifts, axis=1), parts

    def run_steps(step):
        def far(shift):
            def body(t, carry):
                step(t, shift=shift)
                return carry
            return body

        lax.fori_loop(0, jnp.clip(j_lo, 0, nk), far(c_left), 0)
        for i in range(n_band):
            j = j_lo + i

            @pl.when((j >= 0) & (j < nk))
            def _():
                shift, parts = band_bias(i)
                step(j, shift=shift, bias=parts)
        lax.fori_loop(jnp.clip(j_lo + n_band, 0, nk), nk, far(c_right), 0)

    _attn_pipeline(k_ref, vt_ref, qt_ref, scr, run_steps, nk=nk, tk=tk, use_shift=True)

    lp = lamp_ref[...]
    lam = (jnp.exp(jnp.sum(lp[0:1] * lp[1:2], keepdims=True))
           - jnp.exp(jnp.sum(lp[2:3] * lp[3:4], keepdims=True)) + lam_init)
    acc = scr[1][...]
    o1 = acc[0:C_V, 0:tq] / acc[ONES_COL:ONES_COL + 1, 0:tq]
    o2 = acc[0:C_V, tq:2 * tq] / acc[ONES_COL:ONES_COL + 1, tq:2 * tq]
    o = (o1 - lam * o2).T
    o_ref[0, 0] = (_rms(o, g_ref[...]) * (1.0 - lam_init)).astype(BF16)


def _attn_common(nb, seq, tq, tk, n_cols):
    grid = (nb, B_HEADS, seq // tq)
    q_spec = pl.BlockSpec((1, 1, tq, HEAD_PAD), lambda b, h, i: (b, h, i, 0))
    kv_spec = pl.BlockSpec((1, 1, seq, HEAD_PAD), lambda b, h, i: (b, h, 0, 0))
    out_spec = pl.BlockSpec((1, 1, tq, B_V), lambda b, h, i: (b, h, i, 0))
    out_shape = jax.ShapeDtypeStruct((nb, B_HEADS, seq, B_V), BF16)
    scratch = [pltpu.VMEM((seq // tk, HEAD_PAD, tk), BF16),
               pltpu.VMEM((HEAD_PAD, n_cols), BF16),
               pltpu.VMEM((1, n_cols), F32),
               pltpu.VMEM((PV_ROWS, n_cols), F32),
               pltpu.VMEM((tk, n_cols), F32),
               pltpu.VMEM((1, n_cols), F32),
               pltpu.VMEM((1, n_cols), F32),
               pltpu.VMEM((tk, n_cols), BF16),
               pltpu.VMEM((1, n_cols), F32)]
    params = pltpu.CompilerParams(dimension_semantics=("arbitrary",) * 3,
                                  vmem_limit_bytes=VMEM_LIMIT)
    return grid, q_spec, kv_spec, out_spec, out_shape, scratch, params


def _mla_attn(q, k, v, t):
    nb, _, seq, _ = q.shape
    tq, tk = t["tq_mla"], t["tk"]
    grid, q_spec, kv_spec, out_spec, out_shape, scratch, params = _attn_common(nb, seq, tq, tk, tq)
    return pl.pallas_call(
        functools.partial(_mla_attn_kernel, seq=seq, tq=tq, tk=tk, vchunk=t["vchunk"]),
        grid=grid, in_specs=[q_spec, kv_spec, kv_spec], out_specs=out_spec, out_shape=out_shape,
        scratch_shapes=scratch, compiler_params=params, name="mla_attn",
    )(q, k, v)


def _diff_attn(q, k, v, bias_tiles, cfar, lam_params, subln_g, lam_init, t):
    nb, _, seq, _ = q.shape
    tq, tk = t["tq_diff"], t["tk"]
    grid, q_spec, kv_spec, out_spec, out_shape, scratch, params = _attn_common(nb, seq, tq, tk, 2 * tq)
    return pl.pallas_call(
        functools.partial(_diff_attn_kernel, seq=seq, tq=tq, tk=tk, vchunk=t["vchunk"],
                          lam_init=lam_init),
        grid=grid,
        in_specs=[q_spec, kv_spec, kv_spec,
                  pl.BlockSpec((1, 3, tk, tk), lambda b, h, i: (h, 0, 0, 0)),
                  pl.BlockSpec(memory_space=pltpu.SMEM),
                  pl.BlockSpec((8, LANES), lambda b, h, i: (0, 0)),
                  pl.BlockSpec((1, C_V), lambda b, h, i: (0, 0))],
        out_specs=out_spec, out_shape=out_shape, scratch_shapes=scratch, compiler_params=params,
        name="diff_attn",
    )(q, k, v, bias_tiles, cfar, lam_params, subln_g)


def _bias_kernel(bucket_ref, rb_ref, o_ref):
    hd = pl.program_id(0)
    bucket = bucket_ref[0]
    acc = jnp.zeros(bucket.shape, F32)
    for b in range(REL_BUCKETS):
        acc = jnp.where(bucket == b, rb_ref[b, hd], acc)
    o_ref[0, 0] = acc * LOG2E


def _bias_tiles(bucket_t, rel_bias):
    n_tiles, tk, tq = bucket_t.shape
    return pl.pallas_call(
        _bias_kernel,
        grid=(C_HEADS, n_tiles),
        in_specs=[pl.BlockSpec((1, tk, tq), lambda h, i: (i, 0, 0)),
                  pl.BlockSpec(memory_space=pltpu.SMEM)],
        out_specs=pl.BlockSpec((1, 1, tk, tq), lambda h, i: (h, i, 0, 0)),
        out_shape=jax.ShapeDtypeStruct((C_HEADS, n_tiles, tk, tq), F32),
        name="bias_tiles",
    )(bucket_t, rel_bias)


def _post_kernel(x_ref, mixa_ref, ob_ref, oc_ref, wout_ref, mod_ref, n2g_ref, rw_ref, rb_ref,
                 x2_ref, h2_ref, idx_ref, gate_ref, rank_ref, cnt_ref, mix_scr, base_scr, *, tm):
    @pl.when((pl.program_id(0) == 0) & (pl.program_id(1) == 0))
    def _():
        base_scr[...] = jnp.zeros(base_scr.shape, F32)

    mix_scr[:, 0:A_WIDTH] = mixa_ref[0]
    for hd in range(B_HEADS):
        mix_scr[:, A_WIDTH + hd * B_V:A_WIDTH + (hd + 1) * B_V] = ob_ref[0, hd]
    off_c = A_WIDTH + B_HEADS * B_V
    for hd in range(C_HEADS):
        mix_scr[:, off_c + hd * C_V:off_c + (hd + 1) * C_V] = oc_ref[0, hd]
    y = jnp.dot(mix_scr[...], wout_ref[...], preferred_element_type=F32)
    mod = mod_ref[0]
    x2 = x_ref[0] + mod[2:3] * y
    x2_ref[0] = x2
    h2 = _rms(x2, n2g_ref[...]) * (1.0 + mod[4:5]) + mod[3:4]
    h2_ref[0] = h2

    logits = jnp.dot(h2, rw_ref[...], precision=HIGHEST, preferred_element_type=F32) + rb_ref[...]
    lane = lax.broadcasted_iota(jnp.int32, (tm, LANES), 1)
    vals, idxs = [], []
    cur = logits
    for _ in range(TOP_K):
        mx = jnp.max(cur, axis=1, keepdims=True)
        ik = jnp.min(jnp.where(cur == mx, lane, LANES), axis=1, keepdims=True)
        vals.append(mx)
        idxs.append(ik)
        cur = jnp.where(lane == ik, NEG_BIG, cur)
    ex = [jnp.exp(vk - vals[0]) for vk in vals]
    den = ex[0] + ex[1] + ex[2] + ex[3]

    onehots = [(lane == ik).astype(F32) for ik in idxs]
    esum = onehots[0] + onehots[1] + onehots[2] + onehots[3]
    r_i = lax.broadcasted_iota(jnp.int32, (tm, tm), 0)
    c_i = lax.broadcasted_iota(jnp.int32, (tm, tm), 1)
    lower = jnp.where(r_i > c_i, 1.0, 0.0).astype(BF16)
    before = base_scr[...] + jnp.dot(lower, esum.astype(BF16), preferred_element_type=F32)
    idx_out = jnp.zeros((tm, LANES), jnp.int32)
    gate_out = jnp.zeros((tm, LANES), F32)
    rank_out = jnp.zeros((tm, LANES), jnp.int32)
    for k in range(TOP_K):
        rk = jnp.sum(onehots[k] * before, axis=1, keepdims=True).astype(jnp.int32)
        idx_out = jnp.where(lane == k, idxs[k], idx_out)
        gate_out = jnp.where(lane == k, ex[k] / den, gate_out)
        rank_out = jnp.where(lane == k, rk, rank_out)
    idx_ref[0] = idx_out
    gate_ref[0] = gate_out
    rank_ref[0] = rank_out
    base_new = base_scr[...] + jnp.sum(esum, axis=0, keepdims=True)
    base_scr[...] = base_new
    cnt_ref[...] = base_new


def _post(x, mixa, ob, oc, mod_l, lw, t):
    nb, seq, _ = x.shape
    tm = t["tm"]
    full = lambda a: pl.BlockSpec(a.shape, lambda b, i: (0,) * a.ndim)
    tok = lambda w: pl.BlockSpec((1, tm, w), lambda b, i: (b, i, 0))
    head_in = pl.BlockSpec((1, B_HEADS, tm, B_V), lambda b, i: (b, 0, i, 0))
    return pl.pallas_call(
        functools.partial(_post_kernel, tm=tm),
        grid=(nb, seq // tm),
        in_specs=[tok(D_MODEL), tok(A_WIDTH), head_in, head_in, full(lw["wout"]),
                  pl.BlockSpec((1, 6, D_MODEL), lambda b, i: (b, 0, 0)),
                  full(lw["n2g"]), full(lw["rw"]), full(lw["rb"])],
        out_specs=[tok(D_MODEL), tok(D_MODEL), tok(LANES), tok(LANES), tok(LANES),
                   pl.BlockSpec((1, LANES), lambda b, i: (0, 0))],
        out_shape=[jax.ShapeDtypeStruct((nb, seq, D_MODEL), F32),
                   jax.ShapeDtypeStruct((nb, seq, D_MODEL), F32),
                   jax.ShapeDtypeStruct((nb, seq, LANES), jnp.int32),
                   jax.ShapeDtypeStruct((nb, seq, LANES), F32),
                   jax.ShapeDtypeStruct((nb, seq, LANES), jnp.int32),
                   jax.ShapeDtypeStruct((1, LANES), F32)],
        scratch_shapes=[pltpu.VMEM((tm, D_MODEL), BF16), pltpu.VMEM((1, LANES), F32)],
        compiler_params=pltpu.CompilerParams(dimension_semantics=("arbitrary", "arbitrary"),
                                             vmem_limit_bytes=VMEM_LIMIT),
        name="post",
    )(x, mixa, ob, oc, lw["wout"], mod_l, lw["n2g"], lw["rw"], lw["rb"])


def _row_copy(src, src_row, dst, dst_row, sem):
    return pltpu.make_async_copy(src.at[pl.ds(src_row, 1)], dst.at[pl.ds(dst_row, 1)], sem)


def _dispatch_kernel(pos_ref, h2_ref, xs_ref, sem, *, td):
    def start(r, carry):
        for k in range(TOP_K):
            _row_copy(h2_ref, r, xs_ref, pos_ref[0, 0, r * TOP_K + k], sem).start(priority=k % 2)
        return carry

    lax.fori_loop(0, td, start, 0)
    for k in range(TOP_K):
        pltpu.make_async_copy(h2_ref, xs_ref.at[pl.ds(0, td)], sem).wait()


def _dispatch(pos, h2, n_slots, t):
    n_tok = h2.shape[0]
    td = t["td"]
    return pl.pallas_call(
        functools.partial(_dispatch_kernel, td=td),
        grid=(n_tok // td,),
        in_specs=[pl.BlockSpec((1, 1, td * TOP_K), lambda i: (i, 0, 0), memory_space=pltpu.SMEM),
                  pl.BlockSpec((td, D_MODEL), lambda i: (i, 0))],
        out_specs=pl.BlockSpec(memory_space=pl.ANY),
        out_shape=jax.ShapeDtypeStruct((n_slots, D_MODEL), F32),
        scratch_shapes=[pltpu.SemaphoreType.DMA(())],
        compiler_params=pltpu.CompilerParams(dimension_semantics=("arbitrary",)),
        name="dispatch",
    )(pos.reshape(n_tok // td, 1, td * TOP_K), h2)


def _expert_kernel(blk_e_ref, valid_ref, first_ref, xs_ref, w1_ref, b1_ref, w2_ref, b2_ref, y_ref,
                   w1_scr, w2_scr, *, tmo):
    del blk_e_ref
    i = pl.program_id(0)

    @pl.when(first_ref[i] == 1)
    def _():
        w1_scr[...] = w1_ref[0].astype(BF16)
        w2_scr[...] = w2_ref[0].astype(BF16)

    @pl.when(valid_ref[i] > 0)
    def _():
        row = lax.broadcasted_iota(jnp.int32, (tmo, 1), 0)
        x = jnp.where(row < valid_ref[i], xs_ref[...], 0.0).astype(BF16)
        hmid = jnp.dot(x, w1_scr[...], preferred_element_type=F32) + b1_ref[0]
        gate = jnp.minimum(hmid[:, 0:D_FF], SWIGLU_LIMIT)
        up = jnp.clip(hmid[:, D_FF:2 * D_FF], -SWIGLU_LIMIT, SWIGLU_LIMIT)
        act = (up + 1.0) * (gate * jax.nn.sigmoid(SWIGLU_ALPHA * gate))
        y_ref[...] = jnp.dot(act.astype(BF16), w2_scr[...], preferred_element_type=F32) + b2_ref[0]

    @pl.when(valid_ref[i] == 0)
    def _():
        y_ref[...] = jnp.zeros(y_ref.shape, F32)


def _experts(blk_e, valid, first, xs, w1, b1, w2, b2, layer, t):
    n_slots = xs.shape[0]
    tmo = t["tmo"]
    off = layer * N_EXPERTS
    grid_spec = pltpu.PrefetchScalarGridSpec(
        num_scalar_prefetch=3,
        grid=(n_slots // tmo,),
        in_specs=[pl.BlockSpec((tmo, D_MODEL), lambda i, e, v, f: (i, 0)),
                  pl.BlockSpec((1, D_MODEL, 2 * D_FF), lambda i, e, v, f: (off + e[i], 0, 0)),
                  pl.BlockSpec((1, 1, 2 * D_FF), lambda i, e, v, f: (off + e[i], 0, 0)),
                  pl.BlockSpec((1, D_FF, D_MODEL), lambda i, e, v, f: (off + e[i], 0, 0)),
                  pl.BlockSpec((1, 1, D_MODEL), lambda i, e, v, f: (off + e[i], 0, 0))],
        out_specs=pl.BlockSpec((tmo, D_MODEL), lambda i, e, v, f: (i, 0)),
        scratch_shapes=[pltpu.VMEM((D_MODEL, 2 * D_FF), BF16), pltpu.VMEM((D_FF, D_MODEL), BF16)],
    )
    return pl.pallas_call(
        functools.partial(_expert_kernel, tmo=tmo),
        grid_spec=grid_spec,
        out_shape=jax.ShapeDtypeStruct((n_slots, D_MODEL), F32),
        compiler_params=pltpu.CompilerParams(dimension_semantics=("arbitrary",),
                                             vmem_limit_bytes=VMEM_LIMIT),
        name="experts",
    )(blk_e, valid, first, xs, w1, b1, w2, b2)


def _combine_kernel(pos_ref, gate_ref, x2_ref, mod_ref, fg_ref, y_ref, o_ref, buf, sem, *, tc, final):
    def start(r, carry):
        for k in range(TOP_K):
            _row_copy(y_ref, pos_ref[0, 0, r * TOP_K + k], buf.at[k], r, sem).start(priority=k % 2)
        return carry

    lax.fori_loop(0, tc, start, 0)
    for k in range(TOP_K):
        pltpu.make_async_copy(y_ref.at[pl.ds(0, tc)], buf.at[k], sem).wait()
    gates = gate_ref[...]
    moe = buf[0] * gates[:, 0:1]
    for k in range(1, TOP_K):
        moe = moe + buf[k] * gates[:, k:k + 1]
    out = x2_ref[...] + mod_ref[0][5:6] * moe
    if final:
        out = _rms(out, fg_ref[...])
    o_ref[...] = out


def _combine(pos, gates, x2, mod_l, final_g, y, seq, t, final, tok_range):
    n_tok = x2.shape[0]
    tc = t["tc"]
    per_b = seq // tc
    first_tile = tok_range[0] // tc
    n_out = tok_range[1] - tok_range[0]
    return pl.pallas_call(
        functools.partial(_combine_kernel, tc=tc, final=final),
        grid=(n_out // tc,),
        in_specs=[pl.BlockSpec((1, 1, tc * TOP_K), lambda i: (i + first_tile, 0, 0),
                               memory_space=pltpu.SMEM),
                  pl.BlockSpec((tc, LANES), lambda i: (i + first_tile, 0)),
                  pl.BlockSpec((tc, D_MODEL), lambda i: (i + first_tile, 0)),
                  pl.BlockSpec((1, 6, D_MODEL), lambda i: ((i + first_tile) // per_b, 0, 0)),
                  pl.BlockSpec((1, D_MODEL), lambda i: (0, 0)),
                  pl.BlockSpec(memory_space=pl.ANY)],
        out_specs=pl.BlockSpec((tc, D_MODEL), lambda i: (i, 0)),
        out_shape=jax.ShapeDtypeStruct((n_out, D_MODEL), F32),
        scratch_shapes=[pltpu.VMEM((TOP_K, tc, D_MODEL), F32), pltpu.SemaphoreType.DMA(())],
        compiler_params=pltpu.CompilerParams(dimension_semantics=("arbitrary",),
                                             vmem_limit_bytes=VMEM_LIMIT),
        name="combine",
    )(pos.reshape(n_tok // tc, 1, tc * TOP_K), gates, x2, mod_l, final_g, y)


def _rot_half_cols(w):
    half = w.shape[-1] // 2
    return jnp.concatenate([-w[..., half:], w[..., :half]], -1)


def _pad_heads(w, n_heads, width):
    lead = w.shape[:-1]
    w = w.reshape(lead + (n_heads, width))
    w = jnp.pad(w, [(0, 0)] * len(lead) + [(0, 0), (0, HEAD_PAD - width)])
    return w.reshape(lead + (n_heads * HEAD_PAD,))


def _rope_block(w):
    return jnp.pad(w, [(0, 0)] * (w.ndim - 1) + [(B_NOPE, HEAD_PAD - B_NOPE - B_ROPE)])


def _pack_layer_weights(w_in, mla_w_uq, mla_w_ukv):
    o = [0, 256, 512, 896, 1152, 1184, 1568, 1952, 2336]
    a = w_in[..., o[0]:o[2]]
    cq = w_in[..., o[2]:o[3]]
    ckv = w_in[..., o[3]:o[4]]
    kr = w_in[..., o[4]:o[5]]
    dq, dk, dv = (w_in[..., o[5]:o[6]], w_in[..., o[6]:o[7]], w_in[..., o[7]:o[8]])
    win = jnp.concatenate([a, cq, ckv, _rope_block(kr), _rope_block(_rot_half_cols(kr)),
                           _pad_heads(dq, C_HEADS, 2 * C_QK), _pad_heads(dk, C_HEADS, 2 * C_QK),
                           _pad_heads(dv, C_HEADS, C_V)], -1).astype(BF16)
    lead = mla_w_uq.shape[:-1]
    uq = mla_w_uq.reshape(lead + (B_HEADS, B_NOPE + B_ROPE))
    nope, rp = uq[..., :B_NOPE], uq[..., B_NOPE:]
    zeros32 = jnp.zeros_like(rp)
    wqa = jnp.concatenate([nope, rp, zeros32], -1).reshape(lead + (B_HEADS * HEAD_PAD,))
    wqb = jnp.concatenate([jnp.zeros_like(nope), _rot_half_cols(rp), zeros32], -1)
    wqb = wqb.reshape(lead + (B_HEADS * HEAD_PAD,))
    lead = mla_w_ukv.shape[:-1]
    ukv = mla_w_ukv.reshape(lead + (B_HEADS, B_NOPE + B_V))
    kn, vv = ukv[..., :B_NOPE], ukv[..., B_NOPE:]
    wkn = jnp.concatenate([kn, jnp.zeros_like(kn)], -1).reshape(lead + (B_HEADS * HEAD_PAD,))
    wv = jnp.concatenate([vv, jnp.zeros_like(vv)], -1).reshape(lead + (B_HEADS * HEAD_PAD,))
    return win, wqa.astype(BF16), wqb.astype(BF16), wkn.astype(BF16), wv.astype(BF16)


def _rope_tables(seq):
    pos = jnp.arange(seq, dtype=F32)
    inv = 1.0 / (ROPE_BASE ** (jnp.arange(B_ROPE // 2, dtype=F32) / (B_ROPE // 2)))
    ang = pos[:, None] * inv[None, :]
    cos = jnp.concatenate([jnp.cos(ang)] * 2, -1)
    sin = jnp.concatenate([jnp.sin(ang)] * 2, -1)
    qscale = (B_NOPE + B_ROPE) ** -0.5 * LOG2E
    ones = jnp.ones((seq, B_NOPE), F32)
    pad = jnp.zeros((seq, HEAD_PAD - B_NOPE - B_ROPE), F32)
    return dict(cosq=jnp.concatenate([ones, cos, pad], -1) * qscale,
                sinq=_rope_block(sin) * qscale,
                cosk=_rope_block(cos), sink=_rope_block(sin))


def _t5_bucket(rel):
    half = REL_BUCKETS // 2
    max_exact = half // 2
    base = jnp.where(rel > 0, half, 0)
    n = jnp.abs(rel)
    nf = jnp.maximum(n, 1).astype(F32)
    large = max_exact + (jnp.log(nf / max_exact) / math.log(REL_MAX_DIST / max_exact)
                         * (half - max_exact)).astype(jnp.int32)
    large = jnp.minimum(large, half - 1)
    return base + jnp.where(n < max_exact, n, large)


def _band_buckets(tk):
    kk = jnp.arange(tk)[:, None]
    qq = jnp.arange(tk)[None, :]
    return jnp.stack([_t5_bucket(e * tk + kk - qq) for e in (-1, 0, 1)], 0)


def kernel(x_prompt, x_sample, c_prompt, c_sample, ada_w, ada_b, norm1_g, w_in, gmlp_ln_g, gmlp_ln_b,
           gmlp_ws, gmlp_bs, mla_q_norm_g, mla_w_uq, mla_kv_norm_g, mla_w_ukv, diff_lq1, diff_lk1,
           diff_lq2, diff_lk2, diff_subln_g, w_out, norm2_g, router_w, router_b, moe_w1, moe_b1,
           moe_w2, moe_b2, rel_bias, final_g):
    depth = ada_w.shape[0]
    n_prompt = x_prompt.shape[0]
    assert x_prompt.shape[1] == x_sample.shape[1]
    x = jnp.concatenate([x_prompt, x_sample], 0)
    nb, seq, _ = x.shape
    n_tok = nb * seq
    t = _tiles(seq)
    assert t["tk"] >= REL_MAX_DIST and t["tq_diff"] % t["tk"] == 0 and seq // t["tk"] >= 2

    c = jnp.concatenate([c_prompt, c_sample], 0)
    c_pad = jnp.pad(c, ((0, 8 - nb), (0, 0)))
    mod = _adaln(c_pad, ada_w, ada_b)[:, :nb].reshape(depth, nb, 6, D_MODEL)

    win, wqa, wqb, wkn, wv = _pack_layer_weights(w_in, mla_w_uq, mla_w_ukv)
    ws_stack = gmlp_ws.reshape(depth, A_GROUPS * GMLP_CHUNK, GMLP_CHUNK).astype(BF16)
    bs_tile = jnp.repeat(jnp.swapaxes(gmlp_bs, 1, 2), A_WIDTH // A_GROUPS, axis=2)
    wout16 = w_out.astype(BF16)
    rw_pad = jnp.pad(router_w, ((0, 0), (0, 0), (0, LANES - N_EXPERTS)))
    rb_pad = jnp.pad(router_b, ((0, 0), (0, LANES - N_EXPERTS)), constant_values=NEG_BIG)
    w1_all = moe_w1.reshape(depth * N_EXPERTS, D_MODEL, 2 * D_FF)
    b1_all = moe_b1.reshape(depth * N_EXPERTS, 1, 2 * D_FF)
    w2_all = moe_w2.reshape(depth * N_EXPERTS, D_FF, D_MODEL)
    b2_all = moe_b2.reshape(depth * N_EXPERTS, 1, D_MODEL)
    tabs = _rope_tables(seq)
    bias_tiles = _bias_tiles(_band_buckets(t["tk"]).astype(jnp.int32), rel_bias)
    half = REL_BUCKETS // 2
    cfar = jnp.stack([rel_bias[half - 1], rel_bias[REL_BUCKETS - 1]], -1) * LOG2E
    lam_rows = jnp.stack([diff_lq1, diff_lk1, diff_lq2, diff_lk2], 1)
    lam_rows = jnp.pad(lam_rows, ((0, 0), (0, 4), (0, LANES - C_QK)))

    tmo = t["tmo"]
    n_tiles = -(-(n_tok * TOP_K) // tmo) + N_EXPERTS
    n_slots = n_tiles * tmo
    final_row = final_g.reshape(1, D_MODEL)

    for l in range(depth):
        lw = dict(n1g=norm1_g[l][None], win=win[l], lng=gmlp_ln_g[l][None], lnb=gmlp_ln_b[l][None],
                  ws=ws_stack[l], bs=bs_tile[l], qng=mla_q_norm_g[l][None], wqa=wqa[l], wqb=wqb[l],
                  kvng=mla_kv_norm_g[l][None], wkn=wkn[l], wv=wv[l], wout=wout16[l],
                  n2g=norm2_g[l][None], rw=rw_pad[l], rb=rb_pad[l][None])
        lam_init = 0.8 - 0.6 * math.exp(-0.3 * l)
        mixa, qb, kb, vb, qd, kd, vd = _pre(x, mod[l], lw, tabs, t)
        ob = _mla_attn(qb, kb, vb, t)
        oc = _diff_attn(qd, kd, vd, bias_tiles, cfar, lam_rows[l], diff_subln_g[l][None], lam_init, t)
        x2, h2, idx, gates, rank, cnt = _post(x, mixa, ob, oc, mod[l], lw, t)

        counts = cnt[0, :N_EXPERTS].astype(jnp.int32)
        padded = ((counts + tmo - 1) // tmo) * tmo
        pend = jnp.cumsum(padded)
        pstart = pend - padded
        idx4 = idx.reshape(n_tok, LANES)[:, :TOP_K]
        pos = (jnp.take(pstart, idx4) + rank.reshape(n_tok, LANES)[:, :TOP_K]).reshape(-1)
        tile_start = jnp.arange(n_tiles, dtype=jnp.int32) * tmo
        blk_e = jnp.minimum(jnp.sum(tile_start[:, None] >= pend[None, :], axis=1),
                            N_EXPERTS - 1).astype(jnp.int32)
        valid = jnp.clip(jnp.take(pstart + counts, blk_e) - tile_start, 0, tmo).astype(jnp.int32)
        first = jnp.concatenate([jnp.ones((1,), jnp.int32),
                                 (blk_e[1:] != blk_e[:-1]).astype(jnp.int32)])

        xs = _dispatch(pos, h2.reshape(n_tok, D_MODEL), n_slots, t)
        y = _experts(blk_e, valid, first, xs, w1_all, b1_all, w2_all, b2_all, l, t)
        comb = functools.partial(_combine, pos, gates.reshape(n_tok, LANES), x2.reshape(n_tok, D_MODEL),
                                 mod[l], final_row, y, seq, t)
        if l < depth - 1:
            x = comb(False, (0, n_tok)).reshape(nb, seq, D_MODEL)
    n_p = n_prompt * seq
    return (comb(True, (0, n_p)).reshape(n_prompt, seq, D_MODEL),
            comb(True, (n_p, n_tok)).reshape(nb - n_prompt, seq, D_MODEL))
```

```python
import functools
import math

import jax
import jax.numpy as jnp
from jax import lax
from jax.experimental import pallas as pl
from jax.experimental.pallas import tpu as pltpu

F32 = jnp.float32
BF16 = jnp.bfloat16
HIGHEST = lax.Precision.HIGHEST
LOG2E = 1.4426950408889634

D_MODEL = 1024
EPS = 1e-6
GMLP_CHUNK = 128
A_GROUPS = 4
A_WIDTH = 256
B_HEADS = 6
B_Q_LORA = 384
B_KV_LORA = 256
B_NOPE = 64
B_ROPE = 32
B_V = 64
ROPE_BASE = 10000.0
C_HEADS = 6
C_QK = 32
C_V = 64
REL_BUCKETS = 32
REL_MAX_DIST = 128
N_EXPERTS = 32
TOP_K = 4
D_FF = 1024
SWIGLU_LIMIT = 7.0
SWIGLU_ALPHA = 1.702

LANES = 128
HEAD_PAD = 128
ONES_COL = 64
PV_ROWS = 80
NEG_BIG = -3.0e38
VMEM_LIMIT = 56 * 1024 * 1024

SEG_A = (0, 512)
SEG_CQ = (512, 896)
SEG_CKV = (896, 1152)
SEG_KR = (1152, 1280)
SEG_KROT = (1280, 1408)
SEG_DQ = (1408, 2176)
SEG_DK = (2176, 2944)
SEG_DV = (2944, 3712)
IN_PACKED = 3712


def _tiles(seq):
    return dict(
        tm=min(512, seq),
        tq_mla=min(8192, seq),
        tq_diff=min(4096, seq),
        tk=min(256, seq),
        vchunk=min(512, seq),
        td=min(512, seq),
        tmo=512,
        tc=min(512, seq),
    )


def _rms(x, g):
    return x * lax.rsqrt(jnp.mean(x * x, -1, keepdims=True) + EPS) * g


def _adaln_kernel(c_ref, w_ref, b_ref, o_ref):
    c = c_ref[...]
    cs = c * jax.nn.sigmoid(c)
    o_ref[0] = jnp.dot(cs, w_ref[0], precision=HIGHEST, preferred_element_type=F32) + b_ref[0]


def _adaln(c_pad, ada_w, ada_b):
    depth = ada_w.shape[0]
    nb = 6
    return pl.pallas_call(
        _adaln_kernel,
        grid=(depth, nb),
        in_specs=[
            pl.BlockSpec((8, D_MODEL), lambda l, j: (0, 0)),
            pl.BlockSpec((1, D_MODEL, D_MODEL), lambda l, j: (l, 0, j)),
            pl.BlockSpec((1, 1, D_MODEL), lambda l, j: (l, 0, j)),
        ],
        out_specs=pl.BlockSpec((1, 8, D_MODEL), lambda l, j: (l, 0, j)),
        out_shape=jax.ShapeDtypeStruct((depth, 8, 6 * D_MODEL), F32),
        compiler_params=pltpu.CompilerParams(dimension_semantics=("arbitrary", "arbitrary")),
        name="adaln",
    )(c_pad, ada_w, ada_b.reshape(depth, 1, 6 * D_MODEL))


def _pre_kernel(x_ref, mod_ref, n1g_ref, win_ref, lng_ref, lnb_ref, ws_ref, bs_ref,
                qng_ref, wqa_ref, wqb_ref, kvng_ref, wkn_ref, wv_ref,
                cosq_ref, sinq_ref, cosk_ref, sink_ref,
                mixa_ref, qb_ref, kb_ref, vb_ref, qd_ref, kd_ref, vd_ref, *, tm):
    x = x_ref[0]
    mod = mod_ref[0]
    h = (_rms(x, n1g_ref[...]) * (1.0 + mod[1:2]) + mod[0:1]).astype(BF16)

    def seg(s):
        return jnp.dot(h, win_ref[:, s[0]:s[1]], preferred_element_type=F32)

    za = seg(SEG_A)
    u = jax.nn.gelu(za[:, 0:A_WIDTH])
    v = jax.nn.gelu(za[:, A_WIDTH:2 * A_WIDTH])
    vc = v - jnp.mean(v, -1, keepdims=True)
    v = vc * lax.rsqrt(jnp.mean(vc * vc, -1, keepdims=True) + EPS) * lng_ref[...] + lnb_ref[...]
    v16 = v.astype(BF16)
    grp = lax.broadcasted_iota(jnp.int32, (1, A_WIDTH), 1) // (A_WIDTH // A_GROUPS)
    for n in range(tm // GMLP_CHUNK):
        rows = slice(n * GMLP_CHUNK, (n + 1) * GMLP_CHUNK)
        r = jnp.dot(ws_ref[...], v16[rows], preferred_element_type=F32)
        sv = bs_ref[...]
        for g in range(A_GROUPS):
            sv = sv + jnp.where(grp == g, r[g * GMLP_CHUNK:(g + 1) * GMLP_CHUNK], 0.0)
        mixa_ref[0, rows, :] = (u[rows] * sv).astype(BF16)

    lane = lax.broadcasted_iota(jnp.int32, (1, HEAD_PAD), 1)
    ones_col = jnp.where(lane == ONES_COL, 1.0, 0.0).astype(F32)

    cqn = _rms(seg(SEG_CQ), qng_ref[...]).astype(BF16)
    qa = jnp.dot(cqn, wqa_ref[...], preferred_element_type=F32)
    qb = jnp.dot(cqn, wqb_ref[...], preferred_element_type=F32)
    cosq = cosq_ref[...]
    sinq = sinq_ref[...]
    for hd in range(B_HEADS):
        cols = slice(hd * HEAD_PAD, (hd + 1) * HEAD_PAD)
        qb_ref[0, hd] = (qa[:, cols] * cosq + qb[:, cols] * sinq).astype(BF16)
    ckvn = _rms(seg(SEG_CKV), kvng_ref[...]).astype(BF16)
    kn = jnp.dot(ckvn, wkn_ref[...], preferred_element_type=F32)
    vv = jnp.dot(ckvn, wv_ref[...], preferred_element_type=F32)
    krp = seg(SEG_KR) * cosk_ref[...] + seg(SEG_KROT) * sink_ref[...]
    for hd in range(B_HEADS):
        cols = slice(hd * HEAD_PAD, (hd + 1) * HEAD_PAD)
        kb_ref[0, hd] = (kn[:, cols] + krp).astype(BF16)
        vb_ref[0, hd] = (vv[:, cols] + ones_col).astype(BF16)

    zdq = seg(SEG_DQ) * (C_QK ** -0.5 * LOG2E)
    zdk = seg(SEG_DK)
    zdv = seg(SEG_DV)
    for hd in range(C_HEADS):
        cols = slice(hd * HEAD_PAD, (hd + 1) * HEAD_PAD)
        qd_ref[0, hd] = zdq[:, cols].astype(BF16)
        kd_ref[0, hd] = zdk[:, cols].astype(BF16)
        vd_ref[0, hd] = (zdv[:, cols] + ones_col).astype(BF16)


def _pre(x, mod_l, lw, tabs, t):
    nb, seq, _ = x.shape
    tm = t["tm"]
    full = lambda a: pl.BlockSpec(a.shape, lambda b, i: (0,) * a.ndim)
    tab = pl.BlockSpec((tm, HEAD_PAD), lambda b, i: (i, 0))
    head_out = pl.BlockSpec((1, B_HEADS, tm, HEAD_PAD), lambda b, i: (b, 0, i, 0))
    head_shape = jax.ShapeDtypeStruct((nb, B_HEADS, seq, HEAD_PAD), BF16)
    consts = [lw["n1g"], lw["win"], lw["lng"], lw["lnb"], lw["ws"], lw["bs"],
              lw["qng"], lw["wqa"], lw["wqb"], lw["kvng"], lw["wkn"], lw["wv"]]
    return pl.pallas_call(
        functools.partial(_pre_kernel, tm=tm),
        grid=(nb, seq // tm),
        in_specs=[pl.BlockSpec((1, tm, D_MODEL), lambda b, i: (b, i, 0)),
                  pl.BlockSpec((1, 6, D_MODEL), lambda b, i: (b, 0, 0))]
                 + [full(a) for a in consts] + [tab] * 4,
        out_specs=[pl.BlockSpec((1, tm, A_WIDTH), lambda b, i: (b, i, 0))] + [head_out] * 6,
        out_shape=[jax.ShapeDtypeStruct((nb, seq, A_WIDTH), BF16)] + [head_shape] * 6,
        compiler_params=pltpu.CompilerParams(dimension_semantics=("arbitrary", "arbitrary"),
                                             vmem_limit_bytes=VMEM_LIMIT),
        name="pre",
    )(x, mod_l, *consts, tabs["cosq"], tabs["sinq"], tabs["cosk"], tabs["sink"])


def _transpose_v(v_ref, vt_ref, *, seq, vchunk, tk):
    per = vchunk // tk

    def body(i, carry):
        blk = v_ref[0, 0, pl.ds(pl.multiple_of(i * vchunk, vchunk), vchunk), :]
        blk_t = blk.astype(F32).T.astype(BF16)
        for c in range(per):
            vt_ref[i * per + c] = blk_t[:, c * tk:(c + 1) * tk]
        return carry

    lax.fori_loop(0, seq // vchunk, body, 0)


def _attn_pipeline(k_ref, vt_ref, qt_ref, scr, run_steps, *, nk, tk, use_shift):
    m_ref, acc_ref, s_buf, cm_buf, c_buf, p_buf, a_buf = scr
    m_ref[...] = jnp.full(m_ref.shape, NEG_BIG, F32)
    acc_ref[...] = jnp.zeros(acc_ref.shape, F32)
    s_buf[...] = jnp.full(s_buf.shape, -jnp.inf, F32)
    cm_buf[...] = jnp.full(cm_buf.shape, NEG_BIG, F32)
    c_buf[...] = jnp.zeros(c_buf.shape, F32)
    p_buf[...] = jnp.zeros(p_buf.shape, BF16)
    a_buf[...] = jnp.ones(a_buf.shape, F32)

    def stage_a(t, shift, bias):
        kt = k_ref[0, 0, pl.ds(pl.multiple_of(t * tk, tk), tk), :]
        s = jnp.dot(kt, qt_ref[...], preferred_element_type=F32)
        if bias:
            cols = [s[:, c * tk:(c + 1) * tk] for c in range(s.shape[1] // tk)]
            for c, tile in bias:
                cols[c] = cols[c] + tile
            s = jnp.concatenate(cols, axis=1)
        s_buf[...] = s
        cm = jnp.max(s, axis=0, keepdims=True)
        if use_shift:
            c = jnp.broadcast_to(jnp.asarray(shift, F32), c_buf.shape)
            c_buf[...] = c
            cm = cm + c
        cm_buf[...] = cm

    def stage_b():
        m_old = m_ref[...]
        m_new = jnp.maximum(m_old, cm_buf[...])
        ref = m_new - c_buf[...] if use_shift else m_new
        p_buf[...] = jnp.exp2(s_buf[...] - ref).astype(BF16)
        a_buf[...] = jnp.exp2(m_old - m_new)
        m_ref[...] = m_new

    def stage_c(t):
        pv = jnp.dot(vt_ref[t][0:PV_ROWS], p_buf[...], preferred_element_type=F32)
        acc_ref[...] = acc_ref[...] * a_buf[...] + pv

    def step(t, shift=0.0, bias=None, do_a=True, do_b=True):
        stage_c(jnp.maximum(t - 2, 0))
        if do_b:
            stage_b()
        if do_a:
            stage_a(t, shift, bias)

    run_steps(step)
    step(nk, do_a=False)
    step(nk + 1, do_a=False, do_b=False)


def _mla_attn_kernel(q_ref, k_ref, v_ref, o_ref, vt_ref, qt_ref, *scr, seq, tq, tk, vchunk):
    @pl.when(pl.program_id(2) == 0)
    def _():
        _transpose_v(v_ref, vt_ref, seq=seq, vchunk=vchunk, tk=tk)

    qt_ref[...] = q_ref[0, 0].astype(F32).T.astype(BF16)
    nk = seq // tk

    def run_steps(step):
        def body(t, carry):
            step(t)
            return carry
        lax.fori_loop(0, nk, body, 0)

    _attn_pipeline(k_ref, vt_ref, qt_ref, scr, run_steps, nk=nk, tk=tk, use_shift=False)
    acc = scr[1][...]
    o = acc[0:B_V] / acc[ONES_COL:ONES_COL + 1]
    o_ref[0, 0] = o.T.astype(BF16)


def _diff_attn_kernel(q_ref, k_ref, v_ref, bias_ref, cfar_ref, lamp_ref, g_ref, o_ref, vt_ref, qt_ref,
                      *scr, seq, tq, tk, vchunk, lam_init):
    hd = pl.program_id(1)
    qi = pl.program_id(2)

    @pl.when(qi == 0)
    def _():
        _transpose_v(v_ref, vt_ref, seq=seq, vchunk=vchunk, tk=tk)

    qt = q_ref[0, 0].astype(F32).T
    row = lax.broadcasted_iota(jnp.int32, (HEAD_PAD, 1), 0)
    qt_ref[:, 0:tq] = jnp.where(row < C_QK, qt, 0.0).astype(BF16)
    qt_ref[:, tq:2 * tq] = jnp.where(row < C_QK, 0.0, qt).astype(BF16)

    nk = seq // tk
    chunks = tq // tk
    n_band = chunks + 2
    j_lo = qi * chunks - 1
    c_left = cfar_ref[hd, 0]
    c_right = cfar_ref[hd, 1]

    def band_bias(i):
        shifts, parts = [], []
        for c in range(chunks):
            e = i - 1 - c
            if e <= -2:
                shifts.append(jnp.full((1, tk), c_left, F32))
            elif e >= 2:
                shifts.append(jnp.full((1, tk), c_right, F32))
            else:
                shifts.append(jnp.zeros((1, tk), F32))
                parts += [(c, bias_ref[0, e + 1]), (chunks + c, bias_ref[0, e + 1])]
        return jnp.concatenate(shifts + shifts, axis=1), parts

    def run_steps(step):
        def far(shift):
            def body(t, carry):
                step(t, shift=shift)
                return carry
            return body

        lax.fori_loop(0, jnp.clip(j_lo, 0, nk), far(c_left), 0)
        for i in range(n_band):
            j = j_lo + i

            @pl.when((j >= 0) & (j < nk))
            def _():
                shift, parts = band_bias(i)
                step(j, shift=shift, bias=parts)
        lax.fori_loop(jnp.clip(j_lo + n_band, 0, nk), nk, far(c_right), 0)

    _attn_pipeline(k_ref, vt_ref, qt_ref, scr, run_steps, nk=nk, tk=tk, use_shift=True)

    lp = lamp_ref[...]
    lam = (jnp.exp(jnp.sum(lp[0:1] * lp[1:2], keepdims=True))
           - jnp.exp(jnp.sum(lp[2:3] * lp[3:4], keepdims=True)) + lam_init)
    acc = scr[1][...]
    o1 = acc[0:C_V, 0:tq] / acc[ONES_COL:ONES_COL + 1, 0:tq]
    o2 = acc[0:C_V, tq:2 * tq] / acc[ONES_COL:ONES_COL + 1, tq:2 * tq]
    o = (o1 - lam * o2).T
    o_ref[0, 0] = (_rms(o, g_ref[...]) * (1.0 - lam_init)).astype(BF16)


def _attn_common(nb, seq, tq, tk, n_cols):
    grid = (nb, B_HEADS, seq // tq)
    q_spec = pl.BlockSpec((1, 1, tq, HEAD_PAD), lambda b, h, i: (b, h, i, 0))
    kv_spec = pl.BlockSpec((1, 1, seq, HEAD_PAD), lambda b, h, i: (b, h, 0, 0))
    out_spec = pl.BlockSpec((1, 1, tq, B_V), lambda b, h, i: (b, h, i, 0))
    out_shape = jax.ShapeDtypeStruct((nb, B_HEADS, seq, B_V), BF16)
    scratch = [pltpu.VMEM((seq // tk, HEAD_PAD, tk), BF16),
               pltpu.VMEM((HEAD_PAD, n_cols), BF16),
               pltpu.VMEM((1, n_cols), F32),
               pltpu.VMEM((PV_ROWS, n_cols), F32),
               pltpu.VMEM((tk, n_cols), F32),
               pltpu.VMEM((1, n_cols), F32),
               pltpu.VMEM((1, n_cols), F32),
               pltpu.VMEM((tk, n_cols), BF16),
               pltpu.VMEM((1, n_cols), F32)]
    params = pltpu.CompilerParams(dimension_semantics=("arbitrary",) * 3,
                                  vmem_limit_bytes=VMEM_LIMIT)
    return grid, q_spec, kv_spec, out_spec, out_shape, scratch, params


def _mla_attn(q, k, v, t):
    nb, _, seq, _ = q.shape
    tq, tk = t["tq_mla"], t["tk"]
    grid, q_spec, kv_spec, out_spec, out_shape, scratch, params = _attn_common(nb, seq, tq, tk, tq)
    return pl.pallas_call(
        functools.partial(_mla_attn_kernel, seq=seq, tq=tq, tk=tk, vchunk=t["vchunk"]),
        grid=grid, in_specs=[q_spec, kv_spec, kv_spec], out_specs=out_spec, out_shape=out_shape,
        scratch_shapes=scratch, compiler_params=params, name="mla_attn",
    )(q, k, v)


def _diff_attn(q, k, v, bias_tiles, cfar, lam_params, subln_g, lam_init, t):
    nb, _, seq, _ = q.shape
    tq, tk = t["tq_diff"], t["tk"]
    grid, q_spec, kv_spec, out_spec, out_shape, scratch, params = _attn_common(nb, seq, tq, tk, 2 * tq)
    return pl.pallas_call(
        functools.partial(_diff_attn_kernel, seq=seq, tq=tq, tk=tk, vchunk=t["vchunk"],
                          lam_init=lam_init),
        grid=grid,
        in_specs=[q_spec, kv_spec, kv_spec,
                  pl.BlockSpec((1, 3, tk, tk), lambda b, h, i: (h, 0, 0, 0)),
                  pl.BlockSpec(memory_space=pltpu.SMEM),
                  pl.BlockSpec((8, LANES), lambda b, h, i: (0, 0)),
                  pl.BlockSpec((1, C_V), lambda b, h, i: (0, 0))],
        out_specs=out_spec, out_shape=out_shape, scratch_shapes=scratch, compiler_params=params,
        name="diff_attn",
    )(q, k, v, bias_tiles, cfar, lam_params, subln_g)


def _bias_kernel(bucket_ref, rb_ref, o_ref):
    hd = pl.program_id(0)
    bucket = bucket_ref[0]
    acc = jnp.zeros(bucket.shape, F32)
    for b in range(REL_BUCKETS):
        acc = jnp.where(bucket == b, rb_ref[b, hd], acc)
    o_ref[0, 0] = acc * LOG2E


def _bias_tiles(bucket_t, rel_bias):
    n_tiles, tk, tq = bucket_t.shape
    return pl.pallas_call(
        _bias_kernel,
        grid=(C_HEADS, n_tiles),
        in_specs=[pl.BlockSpec((1, tk, tq), lambda h, i: (i, 0, 0)),
                  pl.BlockSpec(memory_space=pltpu.SMEM)],
        out_specs=pl.BlockSpec((1, 1, tk, tq), lambda h, i: (h, i, 0, 0)),
        out_shape=jax.ShapeDtypeStruct((C_HEADS, n_tiles, tk, tq), F32),
        name="bias_tiles",
    )(bucket_t, rel_bias)


def _post_kernel(x_ref, mixa_ref, ob_ref, oc_ref, wout_ref, mod_ref, n2g_ref, rw_ref, rb_ref,
                 x2_ref, h2_ref, idx_ref, gate_ref, rank_ref, cnt_ref, mix_scr, base_scr, *, tm):
    @pl.when((pl.program_id(0) == 0) & (pl.program_id(1) == 0))
    def _():
        base_scr[...] = jnp.zeros(base_scr.shape, F32)

    mix_scr[:, 0:A_WIDTH] = mixa_ref[0]
    for hd in range(B_HEADS):
        mix_scr[:, A_WIDTH + hd * B_V:A_WIDTH + (hd + 1) * B_V] = ob_ref[0, hd]
    off_c = A_WIDTH + B_HEADS * B_V
    for hd in range(C_HEADS):
        mix_scr[:, off_c + hd * C_V:off_c + (hd + 1) * C_V] = oc_ref[0, hd]
    y = jnp.dot(mix_scr[...], wout_ref[...], preferred_element_type=F32)
    mod = mod_ref[0]
    x2 = x_ref[0] + mod[2:3] * y
    x2_ref[0] = x2
    h2 = _rms(x2, n2g_ref[...]) * (1.0 + mod[4:5]) + mod[3:4]
    h2_ref[0] = h2

    logits = jnp.dot(h2, rw_ref[...], precision=HIGHEST, preferred_element_type=F32) + rb_ref[...]
    lane = lax.broadcasted_iota(jnp.int32, (tm, LANES), 1)
    vals, idxs = [], []
    cur = logits
    for _ in range(TOP_K):
        mx = jnp.max(cur, axis=1, keepdims=True)
        ik = jnp.min(jnp.where(cur == mx, lane, LANES), axis=1, keepdims=True)
        vals.append(mx)
        idxs.append(ik)
        cur = jnp.where(lane == ik, NEG_BIG, cur)
    ex = [jnp.exp(vk - vals[0]) for vk in vals]
    den = ex[0] + ex[1] + ex[2] + ex[3]

    onehots = [(lane == ik).astype(F32) for ik in idxs]
    esum = onehots[0] + onehots[1] + onehots[2] + onehots[3]
    r_i = lax.broadcasted_iota(jnp.int32, (tm, tm), 0)
    c_i = lax.broadcasted_iota(jnp.int32, (tm, tm), 1)
    lower = jnp.where(r_i > c_i, 1.0, 0.0).astype(BF16)
    before = base_scr[...] + jnp.dot(lower, esum.astype(BF16), preferred_element_type=F32)
    idx_out = jnp.zeros((tm, LANES), jnp.int32)
    gate_out = jnp.zeros((tm, LANES), F32)
    rank_out = jnp.zeros((tm, LANES), jnp.int32)
    for k in range(TOP_K):
        rk = jnp.sum(onehots[k] * before, axis=1, keepdims=True).astype(jnp.int32)
        idx_out = jnp.where(lane == k, idxs[k], idx_out)
        gate_out = jnp.where(lane == k, ex[k] / den, gate_out)
        rank_out = jnp.where(lane == k, rk, rank_out)
    idx_ref[0] = idx_out
    gate_ref[0] = gate_out
    rank_ref[0] = rank_out
    base_new = base_scr[...] + jnp.sum(esum, axis=0, keepdims=True)
    base_scr[...] = base_new
    cnt_ref[...] = base_new


def _post(x, mixa, ob, oc, mod_l, lw, t):
    nb, seq, _ = x.shape
    tm = t["tm"]
    full = lambda a: pl.BlockSpec(a.shape, lambda b, i: (0,) * a.ndim)
    tok = lambda w: pl.BlockSpec((1, tm, w), lambda b, i: (b, i, 0))
    head_in = pl.BlockSpec((1, B_HEADS, tm, B_V), lambda b, i: (b, 0, i, 0))
    return pl.pallas_call(
        functools.partial(_post_kernel, tm=tm),
        grid=(nb, seq // tm),
        in_specs=[tok(D_MODEL), tok(A_WIDTH), head_in, head_in, full(lw["wout"]),
                  pl.BlockSpec((1, 6, D_MODEL), lambda b, i: (b, 0, 0)),
                  full(lw["n2g"]), full(lw["rw"]), full(lw["rb"])],
        out_specs=[tok(D_MODEL), tok(D_MODEL), tok(LANES), tok(LANES), tok(LANES),
                   pl.BlockSpec((1, LANES), lambda b, i: (0, 0))],
        out_shape=[jax.ShapeDtypeStruct((nb, seq, D_MODEL), F32),
                   jax.ShapeDtypeStruct((nb, seq, D_MODEL), F32),
                   jax.ShapeDtypeStruct((nb, seq, LANES), jnp.int32),
                   jax.ShapeDtypeStruct((nb, seq, LANES), F32),
                   jax.ShapeDtypeStruct((nb, seq, LANES), jnp.int32),
                   jax.ShapeDtypeStruct((1, LANES), F32)],
        scratch_shapes=[pltpu.VMEM((tm, D_MODEL), BF16), pltpu.VMEM((1, LANES), F32)],
        compiler_params=pltpu.CompilerParams(dimension_semantics=("arbitrary", "arbitrary"),
                                             vmem_limit_bytes=VMEM_LIMIT),
        name="post",
    )(x, mixa, ob, oc, lw["wout"], mod_l, lw["n2g"], lw["rw"], lw["rb"])


def _row_copy(src, src_row, dst, dst_row, sem):
    return pltpu.make_async_copy(src.at[pl.ds(src_row, 1)], dst.at[pl.ds(dst_row, 1)], sem)


def _dispatch_kernel(pos_ref, h2_ref, xs_ref, sem, *, td):
    def start(r, carry):
        for k in range(TOP_K):
            _row_copy(h2_ref, r, xs_ref, pos_ref[0, 0, r * TOP_K + k], sem).start(priority=k % 2)
        return carry

    lax.fori_loop(0, td, start, 0)
    for k in range(TOP_K):
        pltpu.make_async_copy(h2_ref, xs_ref.at[pl.ds(0, td)], sem).wait()


def _dispatch(pos, h2, n_slots, t):
    n_tok = h2.shape[0]
    td = t["td"]
    return pl.pallas_call(
        functools.partial(_dispatch_kernel, td=td),
        grid=(n_tok // td,),
        in_specs=[pl.BlockSpec((1, 1, td * TOP_K), lambda i: (i, 0, 0), memory_space=pltpu.SMEM),
                  pl.BlockSpec((td, D_MODEL), lambda i: (i, 0))],
        out_specs=pl.BlockSpec(memory_space=pl.ANY),
        out_shape=jax.ShapeDtypeStruct((n_slots, D_MODEL), F32),
        scratch_shapes=[pltpu.SemaphoreType.DMA(())],
        compiler_params=pltpu.CompilerParams(dimension_semantics=("arbitrary",)),
        name="dispatch",
    )(pos.reshape(n_tok // td, 1, td * TOP_K), h2)


def _expert_kernel(blk_e_ref, valid_ref, first_ref, xs_ref, w1_ref, b1_ref, w2_ref, b2_ref, y_ref,
                   w1_scr, w2_scr, *, tmo):
    del blk_e_ref
    i = pl.program_id(0)

    @pl.when(first_ref[i] == 1)
    def _():
        w1_scr[...] = w1_ref[0].astype(BF16)
        w2_scr[...] = w2_ref[0].astype(BF16)

    @pl.when(valid_ref[i] > 0)
    def _():
        row = lax.broadcasted_iota(jnp.int32, (tmo, 1), 0)
        x = jnp.where(row < valid_ref[i], xs_ref[...], 0.0).astype(BF16)
        hmid = jnp.dot(x, w1_scr[...], preferred_element_type=F32) + b1_ref[0]
        gate = jnp.minimum(hmid[:, 0:D_FF], SWIGLU_LIMIT)
        up = jnp.clip(hmid[:, D_FF:2 * D_FF], -SWIGLU_LIMIT, SWIGLU_LIMIT)
        act = (up + 1.0) * (gate * jax.nn.sigmoid(SWIGLU_ALPHA * gate))
        y_ref[...] = jnp.dot(act.astype(BF16), w2_scr[...], preferred_element_type=F32) + b2_ref[0]

    @pl.when(valid_ref[i] == 0)
    def _():
        y_ref[...] = jnp.zeros(y_ref.shape, F32)


def _experts(blk_e, valid, first, xs, w1, b1, w2, b2, layer, t):
    n_slots = xs.shape[0]
    tmo = t["tmo"]
    off = layer * N_EXPERTS
    grid_spec = pltpu.PrefetchScalarGridSpec(
        num_scalar_prefetch=3,
        grid=(n_slots // tmo,),
        in_specs=[pl.BlockSpec((tmo, D_MODEL), lambda i, e, v, f: (i, 0)),
                  pl.BlockSpec((1, D_MODEL, 2 * D_FF), lambda i, e, v, f: (off + e[i], 0, 0)),
                  pl.BlockSpec((1, 1, 2 * D_FF), lambda i, e, v, f: (off + e[i], 0, 0)),
                  pl.BlockSpec((1, D_FF, D_MODEL), lambda i, e, v, f: (off + e[i], 0, 0)),
                  pl.BlockSpec((1, 1, D_MODEL), lambda i, e, v, f: (off + e[i], 0, 0))],
        out_specs=pl.BlockSpec((tmo, D_MODEL), lambda i, e, v, f: (i, 0)),
        scratch_shapes=[pltpu.VMEM((D_MODEL, 2 * D_FF), BF16), pltpu.VMEM((D_FF, D_MODEL), BF16)],
    )
    return pl.pallas_call(
        functools.partial(_expert_kernel, tmo=tmo),
        grid_spec=grid_spec,
        out_shape=jax.ShapeDtypeStruct((n_slots, D_MODEL), F32),
        compiler_params=pltpu.CompilerParams(dimension_semantics=("arbitrary",),
                                             vmem_limit_bytes=VMEM_LIMIT),
        name="experts",
    )(blk_e, valid, first, xs, w1, b1, w2, b2)


def _combine_kernel(pos_ref, gate_ref, x2_ref, mod_ref, fg_ref, y_ref, o_ref, buf, sem, *, tc, final):
    def start(r, carry):
        for k in range(TOP_K):
            _row_copy(y_ref, pos_ref[0, 0, r * TOP_K + k], buf.at[k], r, sem).start(priority=k % 2)
        return carry

    lax.fori_loop(0, tc, start, 0)
    for k in range(TOP_K):
        pltpu.make_async_copy(y_ref.at[pl.ds(0, tc)], buf.at[k], sem).wait()
    gates = gate_ref[...]
    moe = buf[0] * gates[:, 0:1]
    for k in range(1, TOP_K):
        moe = moe + buf[k] * gates[:, k:k + 1]
    out = x2_ref[...] + mod_ref[0][5:6] * moe
    if final:
        out = _rms(out, fg_ref[...])
    o_ref[...] = out


def _combine(pos, gates, x2, mod_l, final_g, y, seq, t, final, tok_range):
    n_tok = x2.shape[0]
    tc = t["tc"]
    per_b = seq // tc
    first_tile = tok_range[0] // tc
    n_out = tok_range[1] - tok_range[0]
    return pl.pallas_call(
        functools.partial(_combine_kernel, tc=tc, final=final),
        grid=(n_out // tc,),
        in_specs=[pl.BlockSpec((1, 1, tc * TOP_K), lambda i: (i + first_tile, 0, 0),
                               memory_space=pltpu.SMEM),
                  pl.BlockSpec((tc, LANES), lambda i: (i + first_tile, 0)),
                  pl.BlockSpec((tc, D_MODEL), lambda i: (i + first_tile, 0)),
                  pl.BlockSpec((1, 6, D_MODEL), lambda i: ((i + first_tile) // per_b, 0, 0)),
                  pl.BlockSpec((1, D_MODEL), lambda i: (0, 0)),
                  pl.BlockSpec(memory_space=pl.ANY)],
        out_specs=pl.BlockSpec((tc, D_MODEL), lambda i: (i, 0)),
        out_shape=jax.ShapeDtypeStruct((n_out, D_MODEL), F32),
        scratch_shapes=[pltpu.VMEM((TOP_K, tc, D_MODEL), F32), pltpu.SemaphoreType.DMA(())],
        compiler_params=pltpu.CompilerParams(dimension_semantics=("arbitrary",),
                                             vmem_limit_bytes=VMEM_LIMIT),
        name="combine",
    )(pos.reshape(n_tok // tc, 1, tc * TOP_K), gates, x2, mod_l, final_g, y)


def _rot_half_cols(w):
    half = w.shape[-1] // 2
    return jnp.concatenate([-w[..., half:], w[..., :half]], -1)


def _pad_heads(w, n_heads, width):
    lead = w.shape[:-1]
    w = w.reshape(lead + (n_heads, width))
    w = jnp.pad(w, [(0, 0)] * len(lead) + [(0, 0), (0, HEAD_PAD - width)])
    return w.reshape(lead + (n_heads * HEAD_PAD,))


def _rope_block(w):
    return jnp.pad(w, [(0, 0)] * (w.ndim - 1) + [(B_NOPE, HEAD_PAD - B_NOPE - B_ROPE)])


def _pack_layer_weights(w_in, mla_w_uq, mla_w_ukv):
    o = [0, 256, 512, 896, 1152, 1184, 1568, 1952, 2336]
    a = w_in[..., o[0]:o[2]]
    cq = w_in[..., o[2]:o[3]]
    ckv = w_in[..., o[3]:o[4]]
    kr = w_in[..., o[4]:o[5]]
    dq, dk, dv = (w_in[..., o[5]:o[6]], w_in[..., o[6]:o[7]], w_in[..., o[7]:o[8]])
    win = jnp.concatenate([a, cq, ckv, _rope_block(kr), _rope_block(_rot_half_cols(kr)),
                           _pad_heads(dq, C_HEADS, 2 * C_QK), _pad_heads(dk, C_HEADS, 2 * C_QK),
                           _pad_heads(dv, C_HEADS, C_V)], -1).astype(BF16)
    lead = mla_w_uq.shape[:-1]
    uq = mla_w_uq.reshape(lead + (B_HEADS, B_NOPE + B_ROPE))
    nope, rp = uq[..., :B_NOPE], uq[..., B_NOPE:]
    zeros32 = jnp.zeros_like(rp)
    wqa = jnp.concatenate([nope, rp, zeros32], -1).reshape(lead + (B_HEADS * HEAD_PAD,))
    wqb = jnp.concatenate([jnp.zeros_like(nope), _rot_half_cols(rp), zeros32], -1)
    wqb = wqb.reshape(lead + (B_HEADS * HEAD_PAD,))
    lead = mla_w_ukv.shape[:-1]
    ukv = mla_w_ukv.reshape(lead + (B_HEADS, B_NOPE + B_V))
    kn, vv = ukv[..., :B_NOPE], ukv[..., B_NOPE:]
    wkn = jnp.concatenate([kn, jnp.zeros_like(kn)], -1).reshape(lead + (B_HEADS * HEAD_PAD,))
    wv = jnp.concatenate([vv, jnp.zeros_like(vv)], -1).reshape(lead + (B_HEADS * HEAD_PAD,))
    return win, wqa.astype(BF16), wqb.astype(BF16), wkn.astype(BF16), wv.astype(BF16)


def _rope_tables(seq):
    pos = jnp.arange(seq, dtype=F32)
    inv = 1.0 / (ROPE_BASE ** (jnp.arange(B_ROPE // 2, dtype=F32) / (B_ROPE // 2)))
    ang = pos[:, None] * inv[None, :]
    cos = jnp.concatenate([jnp.cos(ang)] * 2, -1)
    sin = jnp.concatenate([jnp.sin(ang)] * 2, -1)
    qscale = (B_NOPE + B_ROPE) ** -0.5 * LOG2E
    ones = jnp.ones((seq, B_NOPE), F32)
    pad = jnp.zeros((seq, HEAD_PAD - B_NOPE - B_ROPE), F32)
    return dict(cosq=jnp.concatenate([ones, cos, pad], -1) * qscale,
                sinq=_rope_block(sin) * qscale,
                cosk=_rope_block(cos), sink=_rope_block(sin))


def _t5_bucket(rel):
    half = REL_BUCKETS // 2
    max_exact = half // 2
    base = jnp.where(rel > 0, half, 0)
    n = jnp.abs(rel)
    nf = jnp.maximum(n, 1).astype(F32)
    large = max_exact + (jnp.log(nf / max_exact) / math.log(REL_MAX_DIST / max_exact)
                         * (half - max_exact)).astype(jnp.int32)
    large = jnp.minimum(large, half - 1)
    return base + jnp.where(n < max_exact, n, large)


def _band_buckets(tk):
    kk = jnp.arange(tk)[:, None]
    qq = jnp.arange(tk)[None, :]
    return jnp.stack([_t5_bucket(e * tk + kk - qq) for e in (-1, 0, 1)], 0)


def kernel(x_prompt, x_sample, c_prompt, c_sample, ada_w, ada_b, norm1_g, w_in, gmlp_ln_g, gmlp_ln_b,
           gmlp_ws, gmlp_bs, mla_q_norm_g, mla_w_uq, mla_kv_norm_g, mla_w_ukv, diff_lq1, diff_lk1,
           diff_lq2, diff_lk2, diff_subln_g, w_out, norm2_g, router_w, router_b, moe_w1, moe_b1,
           moe_w2, moe_b2, rel_bias, final_g):
    depth = ada_w.shape[0]
    n_prompt = x_prompt.shape[0]
    assert x_prompt.shape[1] == x_sample.shape[1]
    x = jnp.concatenate([x_prompt, x_sample], 0)
    nb, seq, _ = x.shape
    n_tok = nb * seq
    t = _tiles(seq)
    assert t["tk"] >= REL_MAX_DIST and t["tq_diff"] % t["tk"] == 0 and seq // t["tk"] >= 2

    c = jnp.concatenate([c_prompt, c_sample], 0)
    c_pad = jnp.pad(c, ((0, 8 - nb), (0, 0)))
    mod = _adaln(c_pad, ada_w, ada_b)[:, :nb].reshape(depth, nb, 6, D_MODEL)

    win, wqa, wqb, wkn, wv = _pack_layer_weights(w_in, mla_w_uq, mla_w_ukv)
    ws_stack = gmlp_ws.reshape(depth, A_GROUPS * GMLP_CHUNK, GMLP_CHUNK).astype(BF16)
    bs_tile = jnp.repeat(jnp.swapaxes(gmlp_bs, 1, 2), A_WIDTH // A_GROUPS, axis=2)
    wout16 = w_out.astype(BF16)
    rw_pad = jnp.pad(router_w, ((0, 0), (0, 0), (0, LANES - N_EXPERTS)))
    rb_pad = jnp.pad(router_b, ((0, 0), (0, LANES - N_EXPERTS)), constant_values=NEG_BIG)
    w1_all = moe_w1.reshape(depth * N_EXPERTS, D_MODEL, 2 * D_FF)
    b1_all = moe_b1.reshape(depth * N_EXPERTS, 1, 2 * D_FF)
    w2_all = moe_w2.reshape(depth * N_EXPERTS, D_FF, D_MODEL)
    b2_all = moe_b2.reshape(depth * N_EXPERTS, 1, D_MODEL)
    tabs = _rope_tables(seq)
    bias_tiles = _bias_tiles(_band_buckets(t["tk"]).astype(jnp.int32), rel_bias)
    half = REL_BUCKETS // 2
    cfar = jnp.stack([rel_bias[half - 1], rel_bias[REL_BUCKETS - 1]], -1) * LOG2E
    lam_rows = jnp.stack([diff_lq1, diff_lk1, diff_lq2, diff_lk2], 1)
    lam_rows = jnp.pad(lam_rows, ((0, 0), (0, 4), (0, LANES - C_QK)))

    tmo = t["tmo"]
    n_tiles = -(-(n_tok * TOP_K) // tmo) + N_EXPERTS
    n_slots = n_tiles * tmo
    final_row = final_g.reshape(1, D_MODEL)

    for l in range(depth):
        lw = dict(n1g=norm1_g[l][None], win=win[l], lng=gmlp_ln_g[l][None], lnb=gmlp_ln_b[l][None],
                  ws=ws_stack[l], bs=bs_tile[l], qng=mla_q_norm_g[l][None], wqa=wqa[l], wqb=wqb[l],
                  kvng=mla_kv_norm_g[l][None], wkn=wkn[l], wv=wv[l], wout=wout16[l],
                  n2g=norm2_g[l][None], rw=rw_pad[l], rb=rb_pad[l][None])
        lam_init = 0.8 - 0.6 * math.exp(-0.3 * l)
        mixa, qb, kb, vb, qd, kd, vd = _pre(x, mod[l], lw, tabs, t)
        ob = _mla_attn(qb, kb, vb, t)
        oc = _diff_attn(qd, kd, vd, bias_tiles, cfar, lam_rows[l], diff_subln_g[l][None], lam_init, t)
        x2, h2, idx, gates, rank, cnt = _post(x, mixa, ob, oc, mod[l], lw, t)

        counts = cnt[0, :N_EXPERTS].astype(jnp.int32)
        padded = ((counts + tmo - 1) // tmo) * tmo
        pend = jnp.cumsum(padded)
        pstart = pend - padded
        idx4 = idx.reshape(n_tok, LANES)[:, :TOP_K]
        pos = (jnp.take(pstart, idx4) + rank.reshape(n_tok, LANES)[:, :TOP_K]).reshape(-1)
        tile_start = jnp.arange(n_tiles, dtype=jnp.int32) * tmo
        blk_e = jnp.minimum(jnp.sum(tile_start[:, None] >= pend[None, :], axis=1),
                            N_EXPERTS - 1).astype(jnp.int32)
        valid = jnp.clip(jnp.take(pstart + counts, blk_e) - tile_start, 0, tmo).astype(jnp.int32)
        first = jnp.concatenate([jnp.ones((1,), jnp.int32),
                                 (blk_e[1:] != blk_e[:-1]).astype(jnp.int32)])

        xs = _dispatch(pos, h2.reshape(n_tok, D_MODEL), n_slots, t)
        y = _experts(blk_e, valid, first, xs, w1_all, b1_all, w2_all, b2_all, l, t)
        comb = functools.partial(_combine, pos, gates.reshape(n_tok, LANES), x2.reshape(n_tok, D_MODEL),
                                 mod[l], final_row, y, seq, t)
        if l < depth - 1:
            x = comb(False, (0, n_tok)).reshape(nb, seq, D_MODEL)
    n_p = n_prompt * seq
    return (comb(True, (0, n_p)).reshape(n_prompt, seq, D_MODEL),
            comb(True, (n_p, n_tok)).reshape(nb - n_prompt, seq, D_MODEL))
```

```python
import functools
import math

import jax
import jax.numpy as jnp
from jax import lax
from jax.experimental import pallas as pl
from jax.experimental.pallas import tpu as pltpu

F32 = jnp.float32
BF16 = jnp.bfloat16
HIGHEST = lax.Precision.HIGHEST
LOG2E = 1.4426950408889634

D_MODEL = 1024
EPS = 1e-6
GMLP_CHUNK = 128
A_GROUPS = 4
A_WIDTH = 256
B_HEADS = 6
B_Q_LORA = 384
B_KV_LORA = 256
B_NOPE = 64
B_ROPE = 32
B_V = 64
ROPE_BASE = 10000.0
C_HEADS = 6
C_QK = 32
C_V = 64
REL_BUCKETS = 32
REL_MAX_DIST = 128
N_EXPERTS = 32
TOP_K = 4
D_FF = 1024
SWIGLU_LIMIT = 7.0
SWIGLU_ALPHA = 1.702

LANES = 128
HEAD_PAD = 128
ONES_COL = 64
PV_ROWS = 80
NEG_BIG = -3.0e38
VMEM_LIMIT = 56 * 1024 * 1024

SEG_A = (0, 512)
SEG_CQ = (512, 896)
SEG_CKV = (896, 1152)
SEG_KR = (1152, 1280)
SEG_KROT = (1280, 1408)
SEG_DQ = (1408, 2176)
SEG_DK = (2176, 2944)
SEG_DV = (2944, 3712)
IN_PACKED = 3712


def _tiles(seq):
    return dict(
        tm=min(512, seq),
        tq_mla=min(8192, seq),
        tq_diff=min(4096, seq),
        tk=min(256, seq),
        vchunk=min(512, seq),
        tmo=512,
        tc=min(512, seq),
    )


def _rms(x, g):
    return x * lax.rsqrt(jnp.mean(x * x, -1, keepdims=True) + EPS) * g


def _adaln_kernel(c_ref, w_ref, b_ref, o_ref):
    c = c_ref[...]
    cs = c * jax.nn.sigmoid(c)
    o_ref[0] = jnp.dot(cs, w_ref[0], precision=HIGHEST, preferred_element_type=F32) + b_ref[0]


def _adaln(c_pad, ada_w, ada_b):
    depth = ada_w.shape[0]
    nb = 6
    return pl.pallas_call(
        _adaln_kernel,
        grid=(depth, nb),
        in_specs=[
            pl.BlockSpec((8, D_MODEL), lambda l, j: (0, 0)),
            pl.BlockSpec((1, D_MODEL, D_MODEL), lambda l, j: (l, 0, j)),
            pl.BlockSpec((1, 1, D_MODEL), lambda l, j: (l, 0, j)),
        ],
        out_specs=pl.BlockSpec((1, 8, D_MODEL), lambda l, j: (l, 0, j)),
        out_shape=jax.ShapeDtypeStruct((depth, 8, 6 * D_MODEL), F32),
        compiler_params=pltpu.CompilerParams(dimension_semantics=("arbitrary", "arbitrary")),
        name="adaln",
    )(c_pad, ada_w, ada_b.reshape(depth, 1, 6 * D_MODEL))


def _pre_kernel(x_ref, mod_ref, n1g_ref, win_ref, lng_ref, lnb_ref, ws_ref, bs_ref,
                qng_ref, wqa_ref, wqb_ref, kvng_ref, wkn_ref, wv_ref,
                cosq_ref, sinq_ref, cosk_ref, sink_ref,
                mixa_ref, qb_ref, kb_ref, vb_ref, qd_ref, kd_ref, vd_ref, *, tm):
    x = x_ref[0]
    mod = mod_ref[0]
    h = (_rms(x, n1g_ref[...]) * (1.0 + mod[1:2]) + mod[0:1]).astype(BF16)

    def seg(s):
        return jnp.dot(h, win_ref[:, s[0]:s[1]], preferred_element_type=F32)

    za = seg(SEG_A)
    u = jax.nn.gelu(za[:, 0:A_WIDTH])
    v = jax.nn.gelu(za[:, A_WIDTH:2 * A_WIDTH])
    vc = v - jnp.mean(v, -1, keepdims=True)
    v = vc * lax.rsqrt(jnp.mean(vc * vc, -1, keepdims=True) + EPS) * lng_ref[...] + lnb_ref[...]
    v16 = v.astype(BF16)
    grp = lax.broadcasted_iota(jnp.int32, (1, A_WIDTH), 1) // (A_WIDTH // A_GROUPS)
    for n in range(tm // GMLP_CHUNK):
        rows = slice(n * GMLP_CHUNK, (n + 1) * GMLP_CHUNK)
        r = jnp.dot(ws_ref[...], v16[rows], preferred_element_type=F32)
        sv = bs_ref[...]
        for g in range(A_GROUPS):
            sv = sv + jnp.where(grp == g, r[g * GMLP_CHUNK:(g + 1) * GMLP_CHUNK], 0.0)
        mixa_ref[0, rows, :] = (u[rows] * sv).astype(BF16)

    lane = lax.broadcasted_iota(jnp.int32, (1, HEAD_PAD), 1)
    ones_col = jnp.where(lane == ONES_COL, 1.0, 0.0).astype(F32)

    cqn = _rms(seg(SEG_CQ), qng_ref[...]).astype(BF16)
    qa = jnp.dot(cqn, wqa_ref[...], preferred_element_type=F32)
    qb = jnp.dot(cqn, wqb_ref[...], preferred_element_type=F32)
    cosq = cosq_ref[...]
    sinq = sinq_ref[...]
    for hd in range(B_HEADS):
        cols = slice(hd * HEAD_PAD, (hd + 1) * HEAD_PAD)
        qb_ref[0, hd] = (qa[:, cols] * cosq + qb[:, cols] * sinq).astype(BF16)
    ckvn = _rms(seg(SEG_CKV), kvng_ref[...]).astype(BF16)
    kn = jnp.dot(ckvn, wkn_ref[...], preferred_element_type=F32)
    vv = jnp.dot(ckvn, wv_ref[...], preferred_element_type=F32)
    krp = seg(SEG_KR) * cosk_ref[...] + seg(SEG_KROT) * sink_ref[...]
    for hd in range(B_HEADS):
        cols = slice(hd * HEAD_PAD, (hd + 1) * HEAD_PAD)
        kb_ref[0, hd] = (kn[:, cols] + krp).astype(BF16)
        vb_ref[0, hd] = (vv[:, cols] + ones_col).astype(BF16)

    zdq = seg(SEG_DQ) * (C_QK ** -0.5 * LOG2E)
    zdk = seg(SEG_DK)
    zdv = seg(SEG_DV)
    for hd in range(C_HEADS):
        cols = slice(hd * HEAD_PAD, (hd + 1) * HEAD_PAD)
        qd_ref[0, hd] = zdq[:, cols].astype(BF16)
        kd_ref[0, hd] = zdk[:, cols].astype(BF16)
        vd_ref[0, hd] = (zdv[:, cols] + ones_col).astype(BF16)


def _pre(x, mod_l, lw, tabs, t):
    nb, seq, _ = x.shape
    tm = t["tm"]
    full = lambda a: pl.BlockSpec(a.shape, lambda b, i: (0,) * a.ndim)
    tab = pl.BlockSpec((tm, HEAD_PAD), lambda b, i: (i, 0))
    head_out = pl.BlockSpec((1, B_HEADS, tm, HEAD_PAD), lambda b, i: (b, 0, i, 0))
    head_shape = jax.ShapeDtypeStruct((nb, B_HEADS, seq, HEAD_PAD), BF16)
    consts = [lw["n1g"], lw["win"], lw["lng"], lw["lnb"], lw["ws"], lw["bs"],
              lw["qng"], lw["wqa"], lw["wqb"], lw["kvng"], lw["wkn"], lw["wv"]]
    return pl.pallas_call(
        functools.partial(_pre_kernel, tm=tm),
        grid=(nb, seq // tm),
        in_specs=[pl.BlockSpec((1, tm, D_MODEL), lambda b, i: (b, i, 0)),
                  pl.BlockSpec((1, 6, D_MODEL), lambda b, i: (b, 0, 0))]
                 + [full(a) for a in consts] + [tab] * 4,
        out_specs=[pl.BlockSpec((1, tm, A_WIDTH), lambda b, i: (b, i, 0))] + [head_out] * 6,
        out_shape=[jax.ShapeDtypeStruct((nb, seq, A_WIDTH), BF16)] + [head_shape] * 6,
        compiler_params=pltpu.CompilerParams(dimension_semantics=("arbitrary", "arbitrary"),
                                             vmem_limit_bytes=VMEM_LIMIT),
        name="pre",
    )(x, mod_l, *consts, tabs["cosq"], tabs["sinq"], tabs["cosk"], tabs["sink"])


def _transpose_v(v_ref, vt_ref, *, seq, vchunk, tk):
    per = vchunk // tk

    def body(i, carry):
        blk = v_ref[0, 0, pl.ds(pl.multiple_of(i * vchunk, vchunk), vchunk), :]
        blk_t = blk.astype(F32).T.astype(BF16)
        for c in range(per):
            vt_ref[i * per + c] = blk_t[:, c * tk:(c + 1) * tk]
        return carry

    lax.fori_loop(0, seq // vchunk, body, 0)


def _attn_pipeline(k_ref, vt_ref, qt_ref, scr, run_steps, *, nk, tk, use_shift):
    m_ref, acc_ref, s_buf, cm_buf, c_buf, p_buf, a_buf = scr
    m_ref[...] = jnp.full(m_ref.shape, NEG_BIG, F32)
    acc_ref[...] = jnp.zeros(acc_ref.shape, F32)
    s_buf[...] = jnp.full(s_buf.shape, -jnp.inf, F32)
    cm_buf[...] = jnp.full(cm_buf.shape, NEG_BIG, F32)
    c_buf[...] = jnp.zeros(c_buf.shape, F32)
    p_buf[...] = jnp.zeros(p_buf.shape, BF16)
    a_buf[...] = jnp.ones(a_buf.shape, F32)

    def stage_a(t, shift, bias):
        kt = k_ref[0, 0, pl.ds(pl.multiple_of(t * tk, tk), tk), :]
        s = jnp.dot(kt, qt_ref[...], preferred_element_type=F32)
        if bias:
            cols = [s[:, c * tk:(c + 1) * tk] for c in range(s.shape[1] // tk)]
            for c, tile in bias:
                cols[c] = cols[c] + tile
            s = jnp.concatenate(cols, axis=1)
        s_buf[...] = s
        cm = jnp.max(s, axis=0, keepdims=True)
        if use_shift:
            c = jnp.broadcast_to(jnp.asarray(shift, F32), c_buf.shape)
            c_buf[...] = c
            cm = cm + c
        cm_buf[...] = cm

    def stage_b():
        m_old = m_ref[...]
        m_new = jnp.maximum(m_old, cm_buf[...])
        ref = m_new - c_buf[...] if use_shift else m_new
        p_buf[...] = jnp.exp2(s_buf[...] - ref).astype(BF16)
        a_buf[...] = jnp.exp2(m_old - m_new)
        m_ref[...] = m_new

    def stage_c(t):
        pv = jnp.dot(vt_ref[t][0:PV_ROWS], p_buf[...], preferred_element_type=F32)
        acc_ref[...] = acc_ref[...] * a_buf[...] + pv

    def step(t, shift=0.0, bias=None, do_a=True, do_b=True):
        stage_c(jnp.maximum(t - 2, 0))
        if do_b:
            stage_b()
        if do_a:
            stage_a(t, shift, bias)

    run_steps(step)
    step(nk, do_a=False)
    step(nk + 1, do_a=False, do_b=False)


def _mla_attn_kernel(q_ref, k_ref, v_ref, o_ref, vt_ref, qt_ref, *scr, seq, tq, tk, vchunk):
    @pl.when(pl.program_id(2) == 0)
    def _():
        _transpose_v(v_ref, vt_ref, seq=seq, vchunk=vchunk, tk=tk)

    qt_ref[...] = q_ref[0, 0].astype(F32).T.astype(BF16)
    nk = seq // tk

    def run_steps(step):
        def body(t, carry):
            step(t)
            return carry
        lax.fori_loop(0, nk, body, 0)

    _attn_pipeline(k_ref, vt_ref, qt_ref, scr, run_steps, nk=nk, tk=tk, use_shift=False)
    acc = scr[1][...]
    o = acc[0:B_V] / acc[ONES_COL:ONES_COL + 1]
    o_ref[0, 0] = o.T.astype(BF16)


def _diff_attn_kernel(q_ref, k_ref, v_ref, bias_ref, cfar_ref, lamp_ref, g_ref, o_ref, vt_ref, qt_ref,
                      *scr, seq, tq, tk, vchunk, lam_init):
    hd = pl.program_id(1)
    qi = pl.program_id(2)

    @pl.when(qi == 0)
    def _():
        _transpose_v(v_ref, vt_ref, seq=seq, vchunk=vchunk, tk=tk)

    qt = q_ref[0, 0].astype(F32).T
    row = lax.broadcasted_iota(jnp.int32, (HEAD_PAD, 1), 0)
    qt_ref[:, 0:tq] = jnp.where(row < C_QK, qt, 0.0).astype(BF16)
    qt_ref[:, tq:2 * tq] = jnp.where(row < C_QK, 0.0, qt).astype(BF16)

    nk = seq // tk
    chunks = tq // tk
    n_band = chunks + 2
    j_lo = qi * chunks - 1
    c_left = cfar_ref[hd, 0]
    c_right = cfar_ref[hd, 1]

    def band_bias(i):
        shifts, parts = [], []
        for c in range(chunks):
            e = i - 1 - c
            if e <= -2:
                shifts.append(jnp.full((1, tk), c_left, F32))
            elif e >= 2:
                shifts.append(jnp.full((1, tk), c_right, F32))
            else:
                shifts.append(jnp.zeros((1, tk), F32))
                parts += [(c, bias_ref[0, e + 1]), (chunks + c, bias_ref[0, e + 1])]
        return jnp.concatenate(shifts + shifts, axis=1), parts

    def run_steps(step):
        def far(shift):
            def body(t, carry):
                step(t, shift=shift)
                return carry
            return body

        lax.fori_loop(0, jnp.clip(j_lo, 0, nk), far(c_left), 0)
        for i in range(n_band):
            j = j_lo + i

            @pl.when((j >= 0) & (j < nk))
            def _():
                shift, parts = band_bias(i)
                step(j, shift=shift, bias=parts)
        lax.fori_loop(jnp.clip(j_lo + n_band, 0, nk), nk, far(c_right), 0)

    _attn_pipeline(k_ref, vt_ref, qt_ref, scr, run_steps, nk=nk, tk=tk, use_shift=True)

    lp = lamp_ref[...]
    lam = (jnp.exp(jnp.sum(lp[0:1] * lp[1:2], keepdims=True))
           - jnp.exp(jnp.sum(lp[2:3] * lp[3:4], keepdims=True)) + lam_init)
    acc = scr[1][...]
    o1 = acc[0:C_V, 0:tq] / acc[ONES_COL:ONES_COL + 1, 0:tq]
    o2 = acc[0:C_V, tq:2 * tq] / acc[ONES_COL:ONES_COL + 1, tq:2 * tq]
    o = (o1 - lam * o2).T
    o_ref[0, 0] = (_rms(o, g_ref[...]) * (1.0 - lam_init)).astype(BF16)


def _attn_common(nb, seq, tq, tk, n_cols):
    grid = (nb, B_HEADS, seq // tq)
    q_spec = pl.BlockSpec((1, 1, tq, HEAD_PAD), lambda b, h, i: (b, h, i, 0))
    kv_spec = pl.BlockSpec((1, 1, seq, HEAD_PAD), lambda b, h, i: (b, h, 0, 0))
    out_spec = pl.BlockSpec((1, 1, tq, B_V), lambda b, h, i: (b, h, i, 0))
    out_shape = jax.ShapeDtypeStruct((nb, B_HEADS, seq, B_V), BF16)
    scratch = [pltpu.VMEM((seq // tk, HEAD_PAD, tk), BF16),
               pltpu.VMEM((HEAD_PAD, n_cols), BF16),
               pltpu.VMEM((1, n_cols), F32),
               pltpu.VMEM((PV_ROWS, n_cols), F32),
               pltpu.VMEM((tk, n_cols), F32),
               pltpu.VMEM((1, n_cols), F32),
               pltpu.VMEM((1, n_cols), F32),
               pltpu.VMEM((tk, n_cols), BF16),
               pltpu.VMEM((1, n_cols), F32)]
    params = pltpu.CompilerParams(dimension_semantics=("arbitrary",) * 3,
                                  vmem_limit_bytes=VMEM_LIMIT)
    return grid, q_spec, kv_spec, out_spec, out_shape, scratch, params


def _mla_attn(q, k, v, t):
    nb, _, seq, _ = q.shape
    tq, tk = t["tq_mla"], t["tk"]
    grid, q_spec, kv_spec, out_spec, out_shape, scratch, params = _attn_common(nb, seq, tq, tk, tq)
    return pl.pallas_call(
        functools.partial(_mla_attn_kernel, seq=seq, tq=tq, tk=tk, vchunk=t["vchunk"]),
        grid=grid, in_specs=[q_spec, kv_spec, kv_spec], out_specs=out_spec, out_shape=out_shape,
        scratch_shapes=scratch, compiler_params=params, name="mla_attn",
    )(q, k, v)


def _diff_attn(q, k, v, bias_tiles, cfar, lam_params, subln_g, lam_init, t):
    nb, _, seq, _ = q.shape
    tq, tk = t["tq_diff"], t["tk"]
    grid, q_spec, kv_spec, out_spec, out_shape, scratch, params = _attn_common(nb, seq, tq, tk, 2 * tq)
    return pl.pallas_call(
        functools.partial(_diff_attn_kernel, seq=seq, tq=tq, tk=tk, vchunk=t["vchunk"],
                          lam_init=lam_init),
        grid=grid,
        in_specs=[q_spec, kv_spec, kv_spec,
                  pl.BlockSpec((1, 3, tk, tk), lambda b, h, i: (h, 0, 0, 0)),
                  pl.BlockSpec(memory_space=pltpu.SMEM),
                  pl.BlockSpec((8, LANES), lambda b, h, i: (0, 0)),
                  pl.BlockSpec((1, C_V), lambda b, h, i: (0, 0))],
        out_specs=out_spec, out_shape=out_shape, scratch_shapes=scratch, compiler_params=params,
        name="diff_attn",
    )(q, k, v, bias_tiles, cfar, lam_params, subln_g)


def _bias_kernel(bucket_ref, rb_ref, o_ref):
    hd = pl.program_id(0)
    bucket = bucket_ref[0]
    acc = jnp.zeros(bucket.shape, F32)
    for b in range(REL_BUCKETS):
        acc = jnp.where(bucket == b, rb_ref[b, hd], acc)
    o_ref[0, 0] = acc * LOG2E


def _bias_tiles(bucket_t, rel_bias):
    n_tiles, tk, tq = bucket_t.shape
    return pl.pallas_call(
        _bias_kernel,
        grid=(C_HEADS, n_tiles),
        in_specs=[pl.BlockSpec((1, tk, tq), lambda h, i: (i, 0, 0)),
                  pl.BlockSpec(memory_space=pltpu.SMEM)],
        out_specs=pl.BlockSpec((1, 1, tk, tq), lambda h, i: (h, i, 0, 0)),
        out_shape=jax.ShapeDtypeStruct((C_HEADS, n_tiles, tk, tq), F32),
        name="bias_tiles",
    )(bucket_t, rel_bias)


def _post_kernel(x_ref, mixa_ref, ob_ref, oc_ref, wout_ref, mod_ref, n2g_ref, rw_ref, rb_ref,
                 x2_ref, h2_ref, idx_ref, gate_ref, rank_ref, cnt_ref, mix_scr, base_scr, *, tm):
    @pl.when((pl.program_id(0) == 0) & (pl.program_id(1) == 0))
    def _():
        base_scr[...] = jnp.zeros(base_scr.shape, F32)

    mix_scr[:, 0:A_WIDTH] = mixa_ref[0]
    for hd in range(B_HEADS):
        mix_scr[:, A_WIDTH + hd * B_V:A_WIDTH + (hd + 1) * B_V] = ob_ref[0, hd]
    off_c = A_WIDTH + B_HEADS * B_V
    for hd in range(C_HEADS):
        mix_scr[:, off_c + hd * C_V:off_c + (hd + 1) * C_V] = oc_ref[0, hd]
    y = jnp.dot(mix_scr[...], wout_ref[...], preferred_element_type=F32)
    mod = mod_ref[0]
    x2 = x_ref[0] + mod[2:3] * y
    x2_ref[0] = x2
    h2 = _rms(x2, n2g_ref[...]) * (1.0 + mod[4:5]) + mod[3:4]
    h2_ref[0] = h2

    logits = jnp.dot(h2, rw_ref[...], precision=HIGHEST, preferred_element_type=F32) + rb_ref[...]
    lane = lax.broadcasted_iota(jnp.int32, (tm, LANES), 1)
    vals, idxs = [], []
    cur = logits
    for _ in range(TOP_K):
        mx = jnp.max(cur, axis=1, keepdims=True)
        ik = jnp.min(jnp.where(cur == mx, lane, LANES), axis=1, keepdims=True)
        vals.append(mx)
        idxs.append(ik)
        cur = jnp.where(lane == ik, NEG_BIG, cur)
    ex = [jnp.exp(vk - vals[0]) for vk in vals]
    den = ex[0] + ex[1] + ex[2] + ex[3]

    onehots = [(lane == ik).astype(F32) for ik in idxs]
    esum = onehots[0] + onehots[1] + onehots[2] + onehots[3]
    r_i = lax.broadcasted_iota(jnp.int32, (tm, tm), 0)
    c_i = lax.broadcasted_iota(jnp.int32, (tm, tm), 1)
    lower = jnp.where(r_i > c_i, 1.0, 0.0).astype(BF16)
    before = base_scr[...] + jnp.dot(lower, esum.astype(BF16), preferred_element_type=F32)
    idx_out = jnp.zeros((tm, LANES), jnp.int32)
    gate_out = jnp.zeros((tm, LANES), F32)
    rank_out = jnp.zeros((tm, LANES), jnp.int32)
    for k in range(TOP_K):
        rk = jnp.sum(onehots[k] * before, axis=1, keepdims=True).astype(jnp.int32)
        idx_out = jnp.where(lane == k, idxs[k], idx_out)
        gate_out = jnp.where(lane == k, ex[k] / den, gate_out)
        rank_out = jnp.where(lane == k, rk, rank_out)
    idx_ref[0] = idx_out
    gate_ref[0] = gate_out
    rank_ref[0] = rank_out
    base_new = base_scr[...] + jnp.sum(esum, axis=0, keepdims=True)
    base_scr[...] = base_new
    cnt_ref[...] = base_new


def _post(x, mixa, ob, oc, mod_l, lw, t):
    nb, seq, _ = x.shape
    tm = t["tm"]
    full = lambda a: pl.BlockSpec(a.shape, lambda b, i: (0,) * a.ndim)
    tok = lambda w: pl.BlockSpec((1, tm, w), lambda b, i: (b, i, 0))
    head_in = pl.BlockSpec((1, B_HEADS, tm, B_V), lambda b, i: (b, 0, i, 0))
    return pl.pallas_call(
        functools.partial(_post_kernel, tm=tm),
        grid=(nb, seq // tm),
        in_specs=[tok(D_MODEL), tok(A_WIDTH), head_in, head_in, full(lw["wout"]),
                  pl.BlockSpec((1, 6, D_MODEL), lambda b, i: (b, 0, 0)),
                  full(lw["n2g"]), full(lw["rw"]), full(lw["rb"])],
        out_specs=[tok(D_MODEL), tok(D_MODEL), tok(LANES), tok(LANES), tok(LANES),
                   pl.BlockSpec((1, LANES), lambda b, i: (0, 0))],
        out_shape=[jax.ShapeDtypeStruct((nb, seq, D_MODEL), F32),
                   jax.ShapeDtypeStruct((nb, seq, D_MODEL), F32),
                   jax.ShapeDtypeStruct((nb, seq, LANES), jnp.int32),
                   jax.ShapeDtypeStruct((nb, seq, LANES), F32),
                   jax.ShapeDtypeStruct((nb, seq, LANES), jnp.int32),
                   jax.ShapeDtypeStruct((1, LANES), F32)],
        scratch_shapes=[pltpu.VMEM((tm, D_MODEL), BF16), pltpu.VMEM((1, LANES), F32)],
        compiler_params=pltpu.CompilerParams(dimension_semantics=("arbitrary", "arbitrary"),
                                             vmem_limit_bytes=VMEM_LIMIT),
        name="post",
    )(x, mixa, ob, oc, lw["wout"], mod_l, lw["n2g"], lw["rw"], lw["rb"])


def _row_copy(src, src_row, dst, dst_row, sem):
    return pltpu.make_async_copy(src.at[pl.ds(src_row, 1)], dst.at[pl.ds(dst_row, 1)], sem)


def _expert_kernel(blk_e_ref, valid_ref, first_ref, tok_ref, tokn_ref, h2_ref, w1_ref, b1_ref, w2_ref,
                   b2_ref, y_ref, w1_scr, w2_scr, xbuf, sem, *, tmo):
    del blk_e_ref
    i = pl.program_id(0)
    slot = i % 2

    @pl.when(i == 0)
    def _():
        def body(r, carry):
            _row_copy(h2_ref, tok_ref[0, 0, r], xbuf.at[0], r, sem.at[0]).start()
            return carry
        lax.fori_loop(0, tmo, body, 0)

    @pl.when((i == 0) | (valid_ref[jnp.maximum(i - 1, 0)] > 0))
    def _():
        pltpu.make_async_copy(h2_ref.at[pl.ds(0, tmo)], xbuf.at[slot], sem.at[slot]).wait()

    @pl.when(first_ref[i] == 1)
    def _():
        w1_scr[...] = w1_ref[0].astype(BF16)
        w2_scr[...] = w2_ref[0].astype(BF16)

    @pl.when(valid_ref[i] > 0)
    def _():
        for r in range(tmo):
            _row_copy(h2_ref, tokn_ref[0, 0, r], xbuf.at[1 - slot], r, sem.at[1 - slot]).start(priority=r % 2)
        row = lax.broadcasted_iota(jnp.int32, (tmo, 1), 0)
        x = jnp.where(row < valid_ref[i], xbuf[slot], 0.0).astype(BF16)
        hmid = jnp.dot(x, w1_scr[...], preferred_element_type=F32) + b1_ref[0]
        gate = jnp.minimum(hmid[:, 0:D_FF], SWIGLU_LIMIT)
        up = jnp.clip(hmid[:, D_FF:2 * D_FF], -SWIGLU_LIMIT, SWIGLU_LIMIT)
        act = (up + 1.0) * (gate * jax.nn.sigmoid(SWIGLU_ALPHA * gate))
        y_ref[...] = jnp.dot(act.astype(BF16), w2_scr[...], preferred_element_type=F32) + b2_ref[0]

    @pl.when(valid_ref[i] == 0)
    def _():
        y_ref[...] = jnp.zeros(y_ref.shape, F32)


def _experts(blk_e, valid, first, tok_sorted, h2, w1, b1, w2, b2, layer, t):
    n_tiles = tok_sorted.shape[0]
    tmo = t["tmo"]
    n_slots = n_tiles * tmo
    off = layer * N_EXPERTS
    grid_spec = pltpu.PrefetchScalarGridSpec(
        num_scalar_prefetch=3,
        grid=(n_tiles,),
        in_specs=[pl.BlockSpec((1, 1, tmo), lambda i, e, v, f: (i, 0, 0), memory_space=pltpu.SMEM),
                  pl.BlockSpec((1, 1, tmo), lambda i, e, v, f: (jnp.minimum(i + 1, n_tiles - 1), 0, 0),
                               memory_space=pltpu.SMEM),
                  pl.BlockSpec(memory_space=pl.ANY),
                  pl.BlockSpec((1, D_MODEL, 2 * D_FF), lambda i, e, v, f: (off + e[i], 0, 0)),
                  pl.BlockSpec((1, 1, 2 * D_FF), lambda i, e, v, f: (off + e[i], 0, 0)),
                  pl.BlockSpec((1, D_FF, D_MODEL), lambda i, e, v, f: (off + e[i], 0, 0)),
                  pl.BlockSpec((1, 1, D_MODEL), lambda i, e, v, f: (off + e[i], 0, 0))],
        out_specs=pl.BlockSpec((tmo, D_MODEL), lambda i, e, v, f: (i, 0)),
        scratch_shapes=[pltpu.VMEM((D_MODEL, 2 * D_FF), BF16), pltpu.VMEM((D_FF, D_MODEL), BF16),
                        pltpu.VMEM((2, tmo, D_MODEL), F32), pltpu.SemaphoreType.DMA((2,))],
    )
    return pl.pallas_call(
        functools.partial(_expert_kernel, tmo=tmo),
        grid_spec=grid_spec,
        out_shape=jax.ShapeDtypeStruct((n_slots, D_MODEL), F32),
        compiler_params=pltpu.CompilerParams(dimension_semantics=("arbitrary",),
                                             vmem_limit_bytes=VMEM_LIMIT),
        name="experts",
    )(blk_e, valid, first, tok_sorted, tok_sorted, h2, w1, b1, w2, b2)


def _combine_kernel(pos_ref, gate_ref, x2_ref, mod_ref, fg_ref, y_ref, o_ref, buf, sem, *, tc, final):
    def start(r, carry):
        for k in range(TOP_K):
            _row_copy(y_ref, pos_ref[0, 0, r * TOP_K + k], buf.at[k], r, sem).start(priority=k % 2)
        return carry

    lax.fori_loop(0, tc, start, 0)
    for k in range(TOP_K):
        pltpu.make_async_copy(y_ref.at[pl.ds(0, tc)], buf.at[k], sem).wait()
    gates = gate_ref[...]
    moe = buf[0] * gates[:, 0:1]
    for k in range(1, TOP_K):
        moe = moe + buf[k] * gates[:, k:k + 1]
    out = x2_ref[...] + mod_ref[0][5:6] * moe
    if final:
        out = _rms(out, fg_ref[...])
    o_ref[...] = out


def _combine(pos, gates, x2, mod_l, final_g, y, seq, t, final, tok_range):
    n_tok = x2.shape[0]
    tc = t["tc"]
    per_b = seq // tc
    first_tile = tok_range[0] // tc
    n_out = tok_range[1] - tok_range[0]
    return pl.pallas_call(
        functools.partial(_combine_kernel, tc=tc, final=final),
        grid=(n_out // tc,),
        in_specs=[pl.BlockSpec((1, 1, tc * TOP_K), lambda i: (i + first_tile, 0, 0),
                               memory_space=pltpu.SMEM),
                  pl.BlockSpec((tc, LANES), lambda i: (i + first_tile, 0)),
                  pl.BlockSpec((tc, D_MODEL), lambda i: (i + first_tile, 0)),
                  pl.BlockSpec((1, 6, D_MODEL), lambda i: ((i + first_tile) // per_b, 0, 0)),
                  pl.BlockSpec((1, D_MODEL), lambda i: (0, 0)),
                  pl.BlockSpec(memory_space=pl.ANY)],
        out_specs=pl.BlockSpec((tc, D_MODEL), lambda i: (i, 0)),
        out_shape=jax.ShapeDtypeStruct((n_out, D_MODEL), F32),
        scratch_shapes=[pltpu.VMEM((TOP_K, tc, D_MODEL), F32), pltpu.SemaphoreType.DMA(())],
        compiler_params=pltpu.CompilerParams(dimension_semantics=("arbitrary",),
                                             vmem_limit_bytes=VMEM_LIMIT),
        name="combine",
    )(pos.reshape(n_tok // tc, 1, tc * TOP_K), gates, x2, mod_l, final_g, y)


def _rot_half_cols(w):
    half = w.shape[-1] // 2
    return jnp.concatenate([-w[..., half:], w[..., :half]], -1)


def _pad_heads(w, n_heads, width):
    lead = w.shape[:-1]
    w = w.reshape(lead + (n_heads, width))
    w = jnp.pad(w, [(0, 0)] * len(lead) + [(0, 0), (0, HEAD_PAD - width)])
    return w.reshape(lead + (n_heads * HEAD_PAD,))


def _rope_block(w):
    return jnp.pad(w, [(0, 0)] * (w.ndim - 1) + [(B_NOPE, HEAD_PAD - B_NOPE - B_ROPE)])


def _pack_layer_weights(w_in, mla_w_uq, mla_w_ukv):
    o = [0, 256, 512, 896, 1152, 1184, 1568, 1952, 2336]
    a = w_in[..., o[0]:o[2]]
    cq = w_in[..., o[2]:o[3]]
    ckv = w_in[..., o[3]:o[4]]
    kr = w_in[..., o[4]:o[5]]
    dq, dk, dv = (w_in[..., o[5]:o[6]], w_in[..., o[6]:o[7]], w_in[..., o[7]:o[8]])
    win = jnp.concatenate([a, cq, ckv, _rope_block(kr), _rope_block(_rot_half_cols(kr)),
                           _pad_heads(dq, C_HEADS, 2 * C_QK), _pad_heads(dk, C_HEADS, 2 * C_QK),
                           _pad_heads(dv, C_HEADS, C_V)], -1).astype(BF16)
    lead = mla_w_uq.shape[:-1]
    uq = mla_w_uq.reshape(lead + (B_HEADS, B_NOPE + B_ROPE))
    nope, rp = uq[..., :B_NOPE], uq[..., B_NOPE:]
    zeros32 = jnp.zeros_like(rp)
    wqa = jnp.concatenate([nope, rp, zeros32], -1).reshape(lead + (B_HEADS * HEAD_PAD,))
    wqb = jnp.concatenate([jnp.zeros_like(nope), _rot_half_cols(rp), zeros32], -1)
    wqb = wqb.reshape(lead + (B_HEADS * HEAD_PAD,))
    lead = mla_w_ukv.shape[:-1]
    ukv = mla_w_ukv.reshape(lead + (B_HEADS, B_NOPE + B_V))
    kn, vv = ukv[..., :B_NOPE], ukv[..., B_NOPE:]
    wkn = jnp.concatenate([kn, jnp.zeros_like(kn)], -1).reshape(lead + (B_HEADS * HEAD_PAD,))
    wv = jnp.concatenate([vv, jnp.zeros_like(vv)], -1).reshape(lead + (B_HEADS * HEAD_PAD,))
    return win, wqa.astype(BF16), wqb.astype(BF16), wkn.astype(BF16), wv.astype(BF16)


def _rope_tables(seq):
    pos = jnp.arange(seq, dtype=F32)
    inv = 1.0 / (ROPE_BASE ** (jnp.arange(B_ROPE // 2, dtype=F32) / (B_ROPE // 2)))
    ang = pos[:, None] * inv[None, :]
    cos = jnp.concatenate([jnp.cos(ang)] * 2, -1)
    sin = jnp.concatenate([jnp.sin(ang)] * 2, -1)
    qscale = (B_NOPE + B_ROPE) ** -0.5 * LOG2E
    ones = jnp.ones((seq, B_NOPE), F32)
    pad = jnp.zeros((seq, HEAD_PAD - B_NOPE - B_ROPE), F32)
    return dict(cosq=jnp.concatenate([ones, cos, pad], -1) * qscale,
                sinq=_rope_block(sin) * qscale,
                cosk=_rope_block(cos), sink=_rope_block(sin))


def _t5_bucket(rel):
    half = REL_BUCKETS // 2
    max_exact = half // 2
    base = jnp.where(rel > 0, half, 0)
    n = jnp.abs(rel)
    nf = jnp.maximum(n, 1).astype(F32)
    large = max_exact + (jnp.log(nf / max_exact) / math.log(REL_MAX_DIST / max_exact)
                         * (half - max_exact)).astype(jnp.int32)
    large = jnp.minimum(large, half - 1)
    return base + jnp.where(n < max_exact, n, large)


def _band_buckets(tk):
    kk = jnp.arange(tk)[:, None]
    qq = jnp.arange(tk)[None, :]
    return jnp.stack([_t5_bucket(e * tk + kk - qq) for e in (-1, 0, 1)], 0)


def kernel(x_prompt, x_sample, c_prompt, c_sample, ada_w, ada_b, norm1_g, w_in, gmlp_ln_g, gmlp_ln_b,
           gmlp_ws, gmlp_bs, mla_q_norm_g, mla_w_uq, mla_kv_norm_g, mla_w_ukv, diff_lq1, diff_lk1,
           diff_lq2, diff_lk2, diff_subln_g, w_out, norm2_g, router_w, router_b, moe_w1, moe_b1,
           moe_w2, moe_b2, rel_bias, final_g):
    depth = ada_w.shape[0]
    n_prompt = x_prompt.shape[0]
    assert x_prompt.shape[1] == x_sample.shape[1]
    x = jnp.concatenate([x_prompt, x_sample], 0)
    nb, seq, _ = x.shape
    n_tok = nb * seq
    t = _tiles(seq)
    assert t["tk"] >= REL_MAX_DIST and t["tq_diff"] % t["tk"] == 0 and seq // t["tk"] >= 2

    c = jnp.concatenate([c_prompt, c_sample], 0)
    c_pad = jnp.pad(c, ((0, 8 - nb), (0, 0)))
    mod = _adaln(c_pad, ada_w, ada_b)[:, :nb].reshape(depth, nb, 6, D_MODEL)

    win, wqa, wqb, wkn, wv = _pack_layer_weights(w_in, mla_w_uq, mla_w_ukv)
    ws_stack = gmlp_ws.reshape(depth, A_GROUPS * GMLP_CHUNK, GMLP_CHUNK).astype(BF16)
    bs_tile = jnp.repeat(jnp.swapaxes(gmlp_bs, 1, 2), A_WIDTH // A_GROUPS, axis=2)
    wout16 = w_out.astype(BF16)
    rw_pad = jnp.pad(router_w, ((0, 0), (0, 0), (0, LANES - N_EXPERTS)))
    rb_pad = jnp.pad(router_b, ((0, 0), (0, LANES - N_EXPERTS)), constant_values=NEG_BIG)
    w1_all = moe_w1.reshape(depth * N_EXPERTS, D_MODEL, 2 * D_FF)
    b1_all = moe_b1.reshape(depth * N_EXPERTS, 1, 2 * D_FF)
    w2_all = moe_w2.reshape(depth * N_EXPERTS, D_FF, D_MODEL)
    b2_all = moe_b2.reshape(depth * N_EXPERTS, 1, D_MODEL)
    tabs = _rope_tables(seq)
    bias_tiles = _bias_tiles(_band_buckets(t["tk"]).astype(jnp.int32), rel_bias)
    half = REL_BUCKETS // 2
    cfar = jnp.stack([rel_bias[half - 1], rel_bias[REL_BUCKETS - 1]], -1) * LOG2E
    lam_rows = jnp.stack([diff_lq1, diff_lk1, diff_lq2, diff_lk2], 1)
    lam_rows = jnp.pad(lam_rows, ((0, 0), (0, 4), (0, LANES - C_QK)))

    tmo = t["tmo"]
    n_tiles = -(-(n_tok * TOP_K) // tmo) + N_EXPERTS
    n_slots = n_tiles * tmo
    final_row = final_g.reshape(1, D_MODEL)

    for l in range(depth):
        lw = dict(n1g=norm1_g[l][None], win=win[l], lng=gmlp_ln_g[l][None], lnb=gmlp_ln_b[l][None],
                  ws=ws_stack[l], bs=bs_tile[l], qng=mla_q_norm_g[l][None], wqa=wqa[l], wqb=wqb[l],
                  kvng=mla_kv_norm_g[l][None], wkn=wkn[l], wv=wv[l], wout=wout16[l],
                  n2g=norm2_g[l][None], rw=rw_pad[l], rb=rb_pad[l][None])
        lam_init = 0.8 - 0.6 * math.exp(-0.3 * l)
        mixa, qb, kb, vb, qd, kd, vd = _pre(x, mod[l], lw, tabs, t)
        ob = _mla_attn(qb, kb, vb, t)
        oc = _diff_attn(qd, kd, vd, bias_tiles, cfar, lam_rows[l], diff_subln_g[l][None], lam_init, t)
        x2, h2, idx, gates, rank, cnt = _post(x, mixa, ob, oc, mod[l], lw, t)

        counts = cnt[0, :N_EXPERTS].astype(jnp.int32)
        padded = ((counts + tmo - 1) // tmo) * tmo
        pend = jnp.cumsum(padded)
        pstart = pend - padded
        idx4 = idx.reshape(n_tok, LANES)[:, :TOP_K]
        pos = (jnp.take(pstart, idx4) + rank.reshape(n_tok, LANES)[:, :TOP_K]).reshape(-1)
        tile_start = jnp.arange(n_tiles, dtype=jnp.int32) * tmo
        blk_e = jnp.minimum(jnp.sum(tile_start[:, None] >= pend[None, :], axis=1),
                            N_EXPERTS - 1).astype(jnp.int32)
        valid = jnp.clip(jnp.take(pstart + counts, blk_e) - tile_start, 0, tmo).astype(jnp.int32)
        first = jnp.concatenate([jnp.ones((1,), jnp.int32),
                                 (blk_e[1:] != blk_e[:-1]).astype(jnp.int32)])

        tok_by_rank = (jnp.argsort(pos) // TOP_K).astype(jnp.int32)
        e_s = jnp.repeat(blk_e, tmo)
        r_s = jnp.arange(n_slots, dtype=jnp.int32) - jnp.take(pstart, e_s)
        src_rank = jnp.clip(jnp.take(jnp.cumsum(counts) - counts, e_s) + r_s, 0, n_tok * TOP_K - 1)
        tok_sorted = jnp.where(r_s < jnp.take(counts, e_s), jnp.take(tok_by_rank, src_rank), 0)
        y = _experts(blk_e, valid, first, tok_sorted.reshape(n_tiles, 1, tmo), h2.reshape(n_tok, D_MODEL),
                     w1_all, b1_all, w2_all, b2_all, l, t)
        comb = functools.partial(_combine, pos, gates.reshape(n_tok, LANES), x2.reshape(n_tok, D_MODEL),
                                 mod[l], final_row, y, seq, t)
        if l < depth - 1:
            x = comb(False, (0, n_tok)).reshape(nb, seq, D_MODEL)
    n_p = n_prompt * seq
    return (comb(True, (0, n_p)).reshape(n_prompt, seq, D_MODEL),
            comb(True, (n_p, n_tok)).reshape(nb - n_prompt, seq, D_MODEL))
```
